```python
import math
import jax, jax.numpy as jnp
from jax import lax
import numpy as np

D_MODEL = 2048
BATCH = 4
SEQ = 8192
DEPTH = 2

CHUNK = 64
HG_HEADS = 6
HG_DK = 128
HG_DV = 128
HG_W = HG_HEADS * HG_DV
S5_GROUP_CH = 16
S5_W = 512
S5_GROUPS = S5_W // S5_GROUP_CH
S5_STATE = 64
RET_HEADS = 6
RET_DK = 64
RET_DV = 128
RET_W = RET_HEADS * RET_DV
D_MIX = HG_W + S5_W + RET_W
IN_SIZES = (HG_W, HG_W, HG_W, HG_W,
            S5_W,
            RET_HEADS * RET_DK, RET_HEADS * RET_DK,
            RET_W, RET_W)
N_IN = sum(IN_SIZES)
D_FF = 5632
CONV_W = 3
NORM_EPS = 1e-6
ROPE_BASE = 10000.0

kernel_name = "hymba_style_hgrn2_s5_retention_convffn"


def _rms(x):
    xf = x.astype(jnp.float32)
    return xf * lax.rsqrt(jnp.mean(xf * xf, axis=-1, keepdims=True) + NORM_EPS)


def rms_norm(x, gain):
    return (_rms(x) * gain.astype(jnp.float32)).astype(x.dtype)


def hgrn2_mixer(q_raw, f_raw, i_raw, g_raw, lb, onorm):
    B, T, _ = q_raw.shape
    N = T // CHUNK
    f32 = jnp.float32

    def heads(z, d):
        return z.astype(f32).reshape(B, N, CHUNK, HG_HEADS, d).transpose(0, 1, 3, 2, 4)

    q = jax.nn.silu(heads(q_raw, HG_DK))
    fr = heads(f_raw, HG_DK)
    v = heads(i_raw, HG_DV)
    lb = lb.astype(f32).reshape(HG_HEADS, 1, HG_DK)
    log_f = jnp.logaddexp(jnp.log(lb), jnp.log1p(-lb) + jax.nn.log_sigmoid(fr))
    k = (1.0 - lb) * jax.nn.sigmoid(-fr)
    b = jnp.cumsum(log_f, axis=3)
    b_last = b[:, :, :, -1:, :]
    q_dec = q * jnp.exp(b)
    k_dec = k * jnp.exp(b_last - b)
    kv = jnp.einsum('bnhcd,bnhce->bnhde', k_dec, v)
    decay = jnp.exp(b_last[:, :, :, 0, :])

    def step(S, inp):
        dec, kv_n = inp
        return dec[..., None] * S + kv_n, S

    S0 = jnp.zeros((B, HG_HEADS, HG_DK, HG_DV), f32)
    _, S_prev = lax.scan(step, S0, (jnp.moveaxis(decay, 1, 0), jnp.moveaxis(kv, 1, 0)))
    S_prev = jnp.moveaxis(S_prev, 0, 1)
    o_inter = jnp.einsum('bnhcd,bnhde->bnhce', q_dec, S_prev)

    causal = jnp.tril(jnp.ones((CHUNK, CHUNK), dtype=bool))[:, :, None]

    def intra(args):
        qn, kn, vn, bn = args
        diff = bn[:, :, :, None, :] - bn[:, :, None, :, :]
        w = jnp.exp(jnp.where(causal, diff, -jnp.inf))
        A = jnp.einsum('bhtd,bhsd,bhtsd->bhts', qn, kn, w)
        return jnp.einsum('bhts,bhse->bhte', A, vn)

    o_intra = lax.map(intra, (jnp.moveaxis(q, 1, 0), jnp.moveaxis(k, 1, 0),
                              jnp.moveaxis(v, 1, 0), jnp.moveaxis(b, 1, 0)))
    o = o_inter + jnp.moveaxis(o_intra, 0, 1)
    o = o.transpose(0, 1, 3, 2, 4).reshape(B, T, HG_HEADS, HG_DV)
    gate = jax.nn.silu(g_raw.astype(f32).reshape(B, T, HG_HEADS, HG_DV))
    o = _rms(o) * onorm.astype(f32) * gate
    return o.reshape(B, T, HG_W)


def s5_mixer(u, lam_re, lam_im, log_dt, b_re, b_im, c_re, c_im, d_skip, w_glu, b_glu):
    B, T, _ = u.shape
    f32 = jnp.float32
    uf = u.astype(f32).reshape(B, T, S5_GROUPS, S5_GROUP_CH)
    dt = jnp.exp(log_dt.astype(f32))[:, None]
    lr = jnp.minimum(lam_re.astype(f32), -1e-4)
    li = lam_im.astype(f32)
    mag = jnp.exp(lr * dt)
    ar, ai = mag * jnp.cos(li * dt), mag * jnp.sin(li * dt)
    den = lr * lr + li * li
    nr, ni = ar - 1.0, ai
    cr, ci = (nr * lr + ni * li) / den, (ni * lr - nr * li) / den
    br_, bi_ = b_re.astype(f32), b_im.astype(f32)
    bbr = cr[..., None] * br_ - ci[..., None] * bi_
    bbi = cr[..., None] * bi_ + ci[..., None] * br_
    xr = jnp.einsum('btgc,gpc->btgp', uf, bbr)
    xi = jnp.einsum('btgc,gpc->btgp', uf, bbi)
    a_r = jnp.broadcast_to(ar[None, None], (1, T, S5_GROUPS, S5_STATE))
    a_i = jnp.broadcast_to(ai[None, None], (1, T, S5_GROUPS, S5_STATE))

    def combine(e1, e2):
        a1r, a1i, b1r, b1i = e1
        a2r, a2i, b2r, b2i = e2
        return (a2r * a1r - a2i * a1i, a2r * a1i + a2i * a1r,
                a2r * b1r - a2i * b1i + b2r, a2r * b1i + a2i * b1r + b2i)

    _, _, sr, si = lax.associative_scan(combine, (a_r, a_i, xr, xi), axis=1)
    y = (jnp.einsum('btgp,gcp->btgc', sr, c_re.astype(f32))
         - jnp.einsum('btgp,gcp->btgc', si, c_im.astype(f32)))
    y = y.reshape(B, T, S5_W) + d_skip.astype(f32) * uf.reshape(B, T, S5_W)
    y = jax.nn.gelu(y, approximate=False)
    return y * jax.nn.sigmoid(y @ w_glu.astype(f32) + b_glu.astype(f32))


def retention_mixer(q_raw, k_raw, v_raw, g_raw):
    B, T, _ = q_raw.shape
    N = T // CHUNK
    f32 = jnp.float32
    pos = jnp.arange(T, dtype=f32)
    inv_freq = 1.0 / (ROPE_BASE ** jnp.linspace(0.0, 1.0, RET_DK // 2, dtype=f32))
    ang = pos[:, None] * inv_freq[None, :]
    cos, sin = jnp.cos(ang)[:, None, :], jnp.sin(ang)[:, None, :]

    def rope(z):
        z1, z2 = jnp.split(z, 2, axis=-1)
        return jnp.concatenate([z1 * cos - z2 * sin, z2 * cos + z1 * sin], axis=-1)

    q = rope(q_raw.astype(f32).reshape(B, T, RET_HEADS, RET_DK))
    k = rope(k_raw.astype(f32).reshape(B, T, RET_HEADS, RET_DK)) * (RET_DK ** -0.5)
    v = v_raw.astype(f32).reshape(B, T, RET_HEADS, RET_DV)
    q = q.reshape(B, N, CHUNK, RET_HEADS, RET_DK)
    k = k.reshape(B, N, CHUNK, RET_HEADS, RET_DK)
    v = v.reshape(B, N, CHUNK, RET_HEADS, RET_DV)

    log_g = jnp.log(1.0 - 2.0 ** (-5.0 - jnp.arange(RET_HEADS, dtype=f32)))
    idx = jnp.arange(CHUNK, dtype=f32)
    rel = idx[:, None] - idx[None, :]
    Dmat = jnp.where(rel[None] >= 0, jnp.exp(jnp.maximum(rel, 0.0)[None] * log_g[:, None, None]), 0.0)
    scores = jnp.einsum('bnthd,bnshd->bnhts', q, k) * Dmat
    o_intra = jnp.einsum('bnhts,bnshe->bnthe', scores, v)

    q_dec = q * jnp.exp((idx + 1.0)[:, None] * log_g[None, :])[..., None]
    k_dec = k * jnp.exp((CHUNK - 1.0 - idx)[:, None] * log_g[None, :])[..., None]
    kv = jnp.einsum('bnshd,bnshe->bnhde', k_dec, v)
    chunk_decay = jnp.exp(CHUNK * log_g)[None, :, None, None]

    def step(S, kv_n):
        return chunk_decay * S + kv_n, S

    S0 = jnp.zeros((B, RET_HEADS, RET_DK, RET_DV), f32)
    _, S_prev = lax.scan(step, S0, jnp.moveaxis(kv, 1, 0))
    S_prev = jnp.moveaxis(S_prev, 0, 1)
    o_inter = jnp.einsum('bnthd,bnhde->bnthe', q_dec, S_prev)
    o = (o_intra + o_inter).reshape(B, T, RET_HEADS, RET_DV)
    gate = jax.nn.silu(g_raw.astype(f32).reshape(B, T, RET_HEADS, RET_DV))
    return (_rms(o) * gate).reshape(B, T, RET_W)


def conv_gated_ffn(h, w_up, conv_w, conv_b, w_down):
    up = h @ w_up.astype(h.dtype)
    C = up.shape[-1]
    kern = conv_w.astype(up.dtype)[:, None, :]
    up = lax.conv_general_dilated(up, kern, window_strides=(1,), padding=((CONV_W - 1, 0),),
                                  dimension_numbers=('NWC', 'WIO', 'NWC'),
                                  feature_group_count=C) + conv_b.astype(up.dtype)
    gate, val = jnp.split(up, 2, axis=-1)
    return (jax.nn.silu(gate) * val) @ w_down.astype(h.dtype)


def setup_inputs(seed: int = 0) -> dict:
    key = jax.random.key(seed)
    ks = jax.random.split(key, 24)
    f32 = jnp.float32
    nrm = lambda k, s, sc: jax.random.normal(k, s, f32) * sc
    gain = lambda k, s: 1.0 + 0.02 * jax.random.normal(k, s, f32)
    n = jnp.arange(S5_STATE, dtype=f32)
    return {
        "x": jax.random.normal(ks[0], (BATCH, SEQ, D_MODEL), f32),
        "w_in": nrm(ks[1], (DEPTH, D_MODEL, N_IN), D_MODEL ** -0.5),
        "w_out": nrm(ks[2], (DEPTH, D_MIX, D_MODEL), D_MIX ** -0.5),
        "mix_beta": gain(ks[3], (DEPTH, D_MIX)),
        "hg_lb_logits": jax.random.normal(ks[4], (DEPTH, HG_HEADS * HG_DK), f32),
        "hg_onorm": gain(ks[5], (DEPTH, HG_DV)),
        "s5_lam_re": -0.5 + 0.01 * jax.random.normal(ks[6], (DEPTH, S5_GROUPS, S5_STATE), f32),
        "s5_lam_im": math.pi * n + 0.01 * jax.random.normal(ks[7], (DEPTH, S5_GROUPS, S5_STATE), f32),
        "s5_log_dt": jax.random.uniform(ks[8], (DEPTH, S5_GROUPS), f32, math.log(1e-3), math.log(1e-1)),
        "s5_b_re": nrm(ks[9], (DEPTH, S5_GROUPS, S5_STATE, S5_GROUP_CH), (2 * S5_GROUP_CH) ** -0.5),
        "s5_b_im": nrm(ks[10], (DEPTH, S5_GROUPS, S5_STATE, S5_GROUP_CH), (2 * S5_GROUP_CH) ** -0.5),
        "s5_c_re": nrm(ks[11], (DEPTH, S5_GROUPS, S5_GROUP_CH, S5_STATE), S5_STATE ** -0.5),
        "s5_c_im": nrm(ks[12], (DEPTH, S5_GROUPS, S5_GROUP_CH, S5_STATE), S5_STATE ** -0.5),
        "s5_d": jax.random.normal(ks[13], (DEPTH, S5_W), f32),
        "s5_w_glu": nrm(ks[14], (DEPTH, S5_W, S5_W), S5_W ** -0.5),
        "s5_b_glu": nrm(ks[15], (DEPTH, S5_W), 0.01),
        "ffn_w_up": nrm(ks[16], (DEPTH, D_MODEL, 2 * D_FF), D_MODEL ** -0.5),
        "ffn_conv_w": nrm(ks[17], (DEPTH, CONV_W, 2 * D_FF), CONV_W ** -0.5),
        "ffn_conv_b": nrm(ks[18], (DEPTH, 2 * D_FF), 0.01),
        "ffn_w_down": nrm(ks[19], (DEPTH, D_FF, D_MODEL), D_FF ** -0.5),
        "g_pre_mix": gain(ks[20], (DEPTH, D_MODEL)),
        "g_post_mix": gain(ks[21], (DEPTH, D_MODEL)),
        "g_pre_ffn": gain(ks[22], (DEPTH, D_MODEL)),
        "g_post_ffn": gain(ks[23], (DEPTH, D_MODEL)),
    }


def reference(x, w_in, w_out, mix_beta, hg_lb_logits, hg_onorm, s5_lam_re, s5_lam_im, s5_log_dt,
              s5_b_re, s5_b_im, s5_c_re, s5_c_im, s5_d, s5_w_glu, s5_b_glu,
              ffn_w_up, ffn_conv_w, ffn_conv_b, ffn_w_down,
              g_pre_mix, g_post_mix, g_pre_ffn, g_post_ffn):
    p = jax.nn.softmax(hg_lb_logits.astype(jnp.float32), axis=0)
    cum = jnp.cumsum(p, axis=0)
    lower_bounds = cum - cum[0:1]
    split_at = [int(v) for v in np.cumsum(IN_SIZES)[:-1]]
    for l in range(DEPTH):
        h = rms_norm(x, g_pre_mix[l])
        z = h @ w_in[l].astype(h.dtype)
        hq, hf, hi, hg, su, rq, rk, rv, rg = jnp.split(z, split_at, axis=-1)
        o_hg = hgrn2_mixer(hq, hf, hi, hg, lower_bounds[l], hg_onorm[l])
        o_s5 = s5_mixer(su, s5_lam_re[l], s5_lam_im[l], s5_log_dt[l], s5_b_re[l], s5_b_im[l],
                        s5_c_re[l], s5_c_im[l], s5_d[l], s5_w_glu[l], s5_b_glu[l])
        o_rt = retention_mixer(rq, rk, rv, rg)
        mix = jnp.concatenate([o_hg, o_s5, o_rt], axis=-1) * mix_beta[l].astype(jnp.float32)
        y = mix.astype(x.dtype) @ w_out[l].astype(x.dtype)
        x = x + rms_norm(y, g_post_mix[l])
        h = rms_norm(x, g_pre_ffn[l])
        f = conv_gated_ffn(h, ffn_w_up[l], ffn_conv_w[l], ffn_conv_b[l], ffn_w_down[l])
        x = x + rms_norm(f, g_post_ffn[l])
    return x
```

```python
import functools
import math

import jax
import jax.numpy as jnp
from jax import lax
from jax.experimental import pallas as pl
from jax.experimental.pallas import tpu as pltpu

F32 = jnp.float32
BF16 = jnp.bfloat16

D_MODEL = 2048
CHUNK = 64
HG_HEADS = 6
HG_D = 128
HG_W = HG_HEADS * HG_D
S5_GROUP_CH = 16
S5_W = 512
S5_GROUPS = 32
S5_STATE = 64
RET_HEADS = 6
RET_DK = 64
RET_DV = 128
RET_W = RET_HEADS * RET_DV
RET_QK = RET_HEADS * RET_DK
D_MIX = HG_W + S5_W + RET_W
ZH_W = 4 * HG_W
ZR_W = 2 * RET_QK + 2 * RET_W
N_IN = ZH_W + S5_W + ZR_W
D_FF = 5632
NORM_EPS = 1e-6
ROPE_BASE = 10000.0

LANES = 128
VMEM_LIMIT = 60000 * 1024

TM_IN = 256
TM_OUT = 512
TM_FFN = 512
TN_FFN = 512
FFN_HALO = 16
HG_ROWS = 256
HG_SUB = 16
RET_CHUNK = 256
S5_L = 8
S5_R = 256
S5_OCT = 4
S5_OS = 8 * S5_STATE


def _cparams(sem):
    return pltpu.CompilerParams(dimension_semantics=sem, vmem_limit_bytes=VMEM_LIMIT)


def _rms_f32(x):
    return x * lax.rsqrt(jnp.mean(x * x, axis=-1, keepdims=True) + NORM_EPS)


def _inproj_kernel(x_ref, g_ref, w_ref, zh_ref, zu_ref, zr_ref):
    h = (_rms_f32(x_ref[...]) * g_ref[...]).astype(BF16)
    zh_ref[...] = jnp.dot(h, w_ref[:, :ZH_W], preferred_element_type=F32)
    zu_ref[...] = jnp.dot(h, w_ref[:, ZH_W:ZH_W + S5_W], preferred_element_type=F32)
    zr_ref[...] = jnp.dot(h, w_ref[:, ZH_W + S5_W:], preferred_element_type=F32)


def _inproj(x2, gain, w_bf):
    m = x2.shape[0]
    return pl.pallas_call(
        _inproj_kernel,
        grid=(m // TM_IN,),
        in_specs=[
            pl.BlockSpec((TM_IN, D_MODEL), lambda i: (i, 0)),
            pl.BlockSpec((1, D_MODEL), lambda i: (0, 0)),
            pl.BlockSpec((D_MODEL, N_IN), lambda i: (0, 0), pipeline_mode=pl.Buffered(1)),
        ],
        out_specs=[
            pl.BlockSpec((TM_IN, ZH_W), lambda i: (i, 0)),
            pl.BlockSpec((TM_IN, S5_W), lambda i: (i, 0)),
            pl.BlockSpec((TM_IN, ZR_W), lambda i: (i, 0)),
        ],
        out_shape=[
            jax.ShapeDtypeStruct((m, ZH_W), F32),
            jax.ShapeDtypeStruct((m, S5_W), F32),
            jax.ShapeDtypeStruct((m, ZR_W), F32),
        ],
        compiler_params=_cparams(("parallel",)),
        name="inproj",
    )(x2, gain, w_bf)


def _hgrn_kernel(q_ref, f_ref, i_ref, g_ref, lb_ref, on_ref, o_ref, st_ref):
    @pl.when(pl.program_id(2) == 0)
    def _():
        st_ref[...] = jnp.zeros_like(st_ref)

    lb = lb_ref[...]
    log_lb = jnp.log(lb)
    log_1mlb = jnp.log1p(-lb)
    one_mlb = 1.0 - lb
    onorm = on_ref[...]
    tri = (lax.broadcasted_iota(jnp.int32, (CHUNK, CHUNK), 0)
           >= lax.broadcasted_iota(jnp.int32, (CHUNK, CHUNK), 1)).astype(F32)
    m = HG_SUB
    t3 = lax.broadcasted_iota(jnp.int32, (m, m, HG_D), 0)
    s3 = lax.broadcasted_iota(jnp.int32, (m, m, HG_D), 1)
    causal3 = t3 >= s3
    nt = (((1,), (1,)), ((), ()))
    tn = (((0,), (0,)), ((), ()))

    for c in range(HG_ROWS // CHUNK):
        rows = pl.ds(c * CHUNK, CHUNK)
        qr = q_ref[rows, :]
        fr = f_ref[rows, :]
        v = i_ref[rows, :]
        gr = g_ref[rows, :]
        q = qr * jax.nn.sigmoid(qr)
        e = jnp.exp(-jnp.abs(fr))
        log_sig = jnp.minimum(fr, 0.0) - jnp.log1p(e)
        cc = log_1mlb + log_sig
        log_f = jnp.maximum(log_lb, cc) + jnp.log1p(jnp.exp(-jnp.abs(log_lb - cc)))
        k = one_mlb * (jnp.where(fr >= 0.0, e, 1.0) / (1.0 + e))
        b = jnp.dot(tri, log_f, precision=lax.Precision.HIGHEST, preferred_element_type=F32)
        b_last = b[CHUNK - 1:CHUNK, :]
        st = st_ref[...]
        v_bf = v.astype(BF16)
        q_dec = (q * jnp.exp(b)).astype(BF16)
        o_inter = lax.dot_general(q_dec, st.astype(BF16), nt, preferred_element_type=F32)
        k_dec = (k * jnp.exp(b_last - b)).astype(BF16)
        kv_t = lax.dot_general(v_bf, k_dec, tn, preferred_element_type=F32)
        st_ref[...] = st * jnp.exp(b_last) + kv_t

        parts = []
        for blk in range(CHUNK // m):
            lo = blk * m
            b_i = b[lo:lo + m, :]
            q_i = q[lo:lo + m, :]
            k_i = k[lo:lo + m, :]
            w = jnp.exp(jnp.where(causal3, b_i[:, None, :] - b_i[None, :, :], -jnp.inf))
            a_d = jnp.sum(q_i[:, None, :] * k_i[None, :, :] * w, axis=-1)
            o_i = jnp.dot(a_d.astype(BF16), v_bf[lo:lo + m, :], preferred_element_type=F32)
            if blk > 0:
                b_prev = b[lo - 1:lo, :]
                qd = (q_i * jnp.exp(b_i - b_prev)).astype(BF16)
                kd = (k[:lo, :] * jnp.exp(b_prev - b[:lo, :])).astype(BF16)
                a_o = lax.dot_general(qd, kd, nt, preferred_element_type=F32)
                o_i = o_i + jnp.dot(a_o.astype(BF16), v_bf[:lo, :], preferred_element_type=F32)
            parts.append(o_i)
        o = o_inter + jnp.concatenate(parts, axis=0)
        o_ref[rows, :] = _rms_f32(o) * onorm * (gr * jax.nn.sigmoid(gr))


def _hgrn(zh, lb, onorm, batch, seq):
    m = zh.shape[0]
    nr = seq // HG_ROWS

    def col(kk):
        return pl.BlockSpec((HG_ROWS, HG_D), lambda b, h, r: (b * nr + r, kk * HG_HEADS + h))

    return pl.pallas_call(
        _hgrn_kernel,
        grid=(batch, HG_HEADS, nr),
        in_specs=[col(0), col(1), col(2), col(3),
                  pl.BlockSpec((1, HG_D), lambda b, h, r: (0, h)),
                  pl.BlockSpec((1, HG_D), lambda b, h, r: (0, 0))],
        out_specs=pl.BlockSpec((HG_ROWS, HG_D), lambda b, h, r: (b * nr + r, h)),
        out_shape=jax.ShapeDtypeStruct((m, HG_W), F32),
        scratch_shapes=[pltpu.VMEM((HG_D, HG_D), F32)],
        compiler_params=_cparams(("parallel", "parallel", "arbitrary")),
        name="hgrn2",
    )(zh, zh, zh, zh, lb, onorm)


def _s5_tables(lam_re, lam_im, log_dt, b_re, b_im, c_re, c_im):
    hp = lax.Precision.HIGHEST
    L = S5_L
    dt = jnp.exp(log_dt.astype(F32))[:, None]
    lr = jnp.minimum(lam_re.astype(F32), -1e-4)
    li = lam_im.astype(F32)
    mag = jnp.exp(lr * dt)
    ar, ai = mag * jnp.cos(li * dt), mag * jnp.sin(li * dt)
    den = lr * lr + li * li
    nr_, ni_ = ar - 1.0, ai
    cr, ci = (nr_ * lr + ni_ * li) / den, (ni_ * lr - nr_ * li) / den
    br_, bi_ = b_re.astype(F32), b_im.astype(F32)
    bbr = cr[..., None] * br_ - ci[..., None] * bi_
    bbi = cr[..., None] * bi_ + ci[..., None] * br_
    cre, cim = c_re.astype(F32), c_im.astype(F32)
    jj = jnp.arange(L + 1, dtype=F32)[:, None, None]
    pmag = jnp.exp(lr * dt * jj)
    pr, pi = pmag * jnp.cos(li * dt * jj), pmag * jnp.sin(li * dt * jj)

    G, P, CH = S5_GROUPS, S5_STATE, S5_GROUP_CH
    ein = functools.partial(jnp.einsum, precision=hp)
    kj = (ein('gcp,jgp,gpd->jgcd', cre, pr[:L], bbr) - ein('gcp,jgp,gpd->jgcd', cre, pi[:L], bbi)
          - ein('gcp,jgp,gpd->jgcd', cim, pr[:L], bbi) - ein('gcp,jgp,gpd->jgcd', cim, pi[:L], bbr))
    eye8 = jnp.eye(8, dtype=F32)
    kb = jnp.einsum('joicd,ik->joidkc', kj.reshape(L, S5_OCT, 8, CH, CH), eye8)
    kb = kb.reshape(L, S5_OCT, LANES, LANES)
    s_idx = jnp.arange(L)[:, None]
    t_idx = jnp.arange(L)[None, :]
    lag = t_idx - s_idx
    toe = jnp.where((lag >= 0)[:, :, None, None, None], kb[jnp.clip(lag, 0, L - 1)], 0.0)
    w_toe = toe.transpose(2, 0, 3, 1, 4).reshape(S5_OCT, L * LANES, L * LANES)

    prs, pis = pr[L - 1 - jnp.arange(L)], pi[L - 1 - jnp.arange(L)]
    vre = prs[..., None] * bbr[None] - pis[..., None] * bbi[None]
    vim = prs[..., None] * bbi[None] + pis[..., None] * bbr[None]

    def v_lay(z):
        z = z.reshape(L, S5_OCT, 8, P, CH)
        z = jnp.einsum('soipd,ik->soidkp', z, eye8)
        return z.reshape(L, S5_OCT, LANES, S5_OS).transpose(1, 0, 2, 3).reshape(S5_OCT, L * LANES, S5_OS)

    w_in = jnp.concatenate([v_lay(vre), v_lay(vim)], axis=-1)

    prt, pit = pr[1:L + 1], pi[1:L + 1]
    mre = cre[None] * prt[:, :, None, :] - cim[None] * pit[:, :, None, :]
    mim = -(cre[None] * pit[:, :, None, :] + cim[None] * prt[:, :, None, :])

    def m_lay(z):
        z = z.reshape(L, S5_OCT, 8, CH, P)
        z = jnp.einsum('toicp,ik->toipkc', z, eye8)
        return z.reshape(L, S5_OCT, S5_OS, LANES).transpose(1, 2, 0, 3).reshape(S5_OCT, S5_OS, L * LANES)

    w_out = jnp.concatenate([m_lay(mre), m_lay(mim)], axis=1)
    a_l = jnp.concatenate([pr[L].reshape(S5_OCT, 1, S5_OS), pi[L].reshape(S5_OCT, 1, S5_OS)], axis=-1)
    return w_toe.astype(BF16), w_in.astype(BF16), w_out.astype(BF16), a_l


def _s5_kernel(u_ref, wt_ref, wi_ref, wo_ref, al_ref, y_ref, s_ref, v_ref, sp_ref):
    @pl.when(pl.program_id(2) == 0)
    def _():
        s_ref[...] = jnp.zeros_like(s_ref)

    L, R = S5_L, S5_R
    lhs = jnp.concatenate([u_ref[pl.ds(s, R, stride=L), :] for s in range(L)], axis=1).astype(BF16)
    y_intra = jnp.dot(lhs, wt_ref[0], preferred_element_type=F32)
    v_ref[...] = jnp.dot(lhs, wi_ref[0], preferred_element_type=F32)
    ar = al_ref[0, :, :S5_OS]
    ai = al_ref[0, :, S5_OS:]

    def step(n, carry):
        sr, si = carry
        sp_ref[pl.ds(n, 1), :S5_OS] = sr
        sp_ref[pl.ds(n, 1), S5_OS:] = si
        vr = v_ref[pl.ds(n, 1), :S5_OS]
        vi = v_ref[pl.ds(n, 1), S5_OS:]
        return (ar * sr - ai * si + vr, ar * si + ai * sr + vi)

    sr, si = lax.fori_loop(0, R, step, (s_ref[:, :S5_OS], s_ref[:, S5_OS:]), unroll=8)
    s_ref[:, :S5_OS] = sr
    s_ref[:, S5_OS:] = si
    y = y_intra + jnp.dot(sp_ref[...].astype(BF16), wo_ref[0], preferred_element_type=F32)
    for t in range(L):
        y_ref[pl.ds(t, R, stride=L), :] = y[:, t * LANES:(t + 1) * LANES]


def _s5(zu, tables, batch, seq):
    w_toe, w_in, w_out, a_l = tables
    m = zu.shape[0]
    rows = S5_R * S5_L
    nr = seq // rows
    full = lambda shp: pl.BlockSpec((1,) + shp, lambda o, b, r: (o, 0, 0))
    return pl.pallas_call(
        _s5_kernel,
        grid=(S5_OCT, batch, nr),
        in_specs=[pl.BlockSpec((rows, LANES), lambda o, b, r: (b * nr + r, o)),
                  full((S5_L * LANES, S5_L * LANES)), full((S5_L * LANES, 2 * S5_OS)),
                  full((2 * S5_OS, S5_L * LANES)), full((1, 2 * S5_OS))],
        out_specs=pl.BlockSpec((rows, LANES), lambda o, b, r: (b * nr + r, o)),
        out_shape=jax.ShapeDtypeStruct((m, S5_W), F32),
        scratch_shapes=[pltpu.VMEM((1, 2 * S5_OS), F32),
                        pltpu.VMEM((S5_R, 2 * S5_OS), F32),
                        pltpu.VMEM((S5_R, 2 * S5_OS), F32)],
        compiler_params=_cparams(("parallel", "parallel", "arbitrary")),
        name="s5",
    )(zu, w_toe, w_in, w_out, a_l)


def _ret_tables(seq):
    C = RET_CHUNK
    pos = jnp.arange(seq, dtype=F32)
    inv_freq = 1.0 / (ROPE_BASE ** jnp.linspace(0.0, 1.0, RET_DK // 2, dtype=F32))
    ang = pos[:, None] * inv_freq[None, :]
    cos, sin = jnp.cos(ang), jnp.sin(ang)
    cos_t = jnp.concatenate([cos, cos, cos, cos], axis=1)
    sin_t = jnp.concatenate([-sin, sin, -sin, sin], axis=1)
    log_g = jnp.log(1.0 - 2.0 ** (-5.0 - jnp.arange(RET_HEADS, dtype=F32)))
    idx = jnp.arange(C, dtype=F32)
    rel = idx[:, None] - idx[None, :]
    dmat = jnp.where(rel[None] >= 0, jnp.exp(jnp.maximum(rel, 0.0)[None] * log_g[:, None, None]), 0.0)
    lane = jnp.arange(LANES)
    head_mask = ((lane[None, :] // RET_DK) == (jnp.arange(RET_HEADS) % 2)[:, None]).astype(F32)
    gq = jnp.exp((idx + 1.0)[None, :] * log_g[:, None])
    gk = jnp.exp((C - 1.0 - idx)[None, :] * log_g[:, None])
    qm = gq[:, :, None] * head_mask[:, None, :]
    km = gk[:, :, None] * head_mask[:, None, :]
    gc = jnp.exp(C * log_g)
    gc_t = jnp.broadcast_to(gc[:, None, None], (RET_HEADS, 1, LANES))
    return cos_t, sin_t, dmat, head_mask[:, None, :], qm, km, gc_t


def _ret_kernel(q_ref, k_ref, v_ref, g_ref, cos_ref, sin_ref, dm_ref, hm_ref, qm_ref, km_ref, gc_ref,
                o_ref, st_ref):
    @pl.when(pl.program_id(1) == 0)
    def _():
        st_ref[...] = jnp.zeros_like(st_ref)

    cos_t = cos_ref[...]
    sin_t = sin_ref[...]
    lane = lax.broadcasted_iota(jnp.int32, (RET_CHUNK, LANES), 1)
    first_half = (lane % RET_DK) < (RET_DK // 2)
    nt = (((1,), (1,)), ((), ()))
    tn = (((0,), (0,)), ((), ()))

    def rope(z):
        swapped = jnp.where(first_half, pltpu.roll(z, LANES - RET_DK // 2, 1), pltpu.roll(z, RET_DK // 2, 1))
        return z * cos_t + swapped * sin_t

    for tile in range(RET_QK // LANES):
        cols = slice(tile * LANES, (tile + 1) * LANES)
        q_t = rope(q_ref[:, cols])
        k_t = rope(k_ref[:, cols]) * (RET_DK ** -0.5)
        k_bf = k_t.astype(BF16)
        for half in range(2):
            h = 2 * tile + half
            vcols = slice(h * RET_DV, (h + 1) * RET_DV)
            v_bf = v_ref[:, vcols].astype(BF16)
            q_h = (q_t * hm_ref[h]).astype(BF16)
            scores = lax.dot_general(q_h, k_bf, nt, preferred_element_type=F32) * dm_ref[h]
            o = jnp.dot(scores.astype(BF16), v_bf, preferred_element_type=F32)
            st = st_ref[h]
            o = o + jnp.dot((q_t * qm_ref[h]).astype(BF16), st.astype(BF16), preferred_element_type=F32)
            k_dec = (k_t * km_ref[h]).astype(BF16)
            st_ref[h] = st * gc_ref[h] + lax.dot_general(k_dec, v_bf, tn, preferred_element_type=F32)
            gr = g_ref[:, vcols]
            o_ref[:, vcols] = _rms_f32(o) * (gr * jax.nn.sigmoid(gr))


def _ret(zr, tables, batch, seq):
    cos_t, sin_t, dmat, hmask, qm, km, gc_t = tables
    m = zr.shape[0]
    C = RET_CHUNK
    nr = seq // C
    const3 = lambda a: pl.BlockSpec(a.shape, lambda b, r: (0, 0, 0))
    return pl.pallas_call(
        _ret_kernel,
        grid=(batch, nr),
        in_specs=[pl.BlockSpec((C, RET_QK), lambda b, r: (b * nr + r, 0)),
                  pl.BlockSpec((C, RET_QK), lambda b, r: (b * nr + r, 1)),
                  pl.BlockSpec((C, RET_W), lambda b, r: (b * nr + r, 1)),
                  pl.BlockSpec((C, RET_W), lambda b, r: (b * nr + r, 2)),
                  pl.BlockSpec((C, LANES), lambda b, r: (r, 0)),
                  pl.BlockSpec((C, LANES), lambda b, r: (r, 0)),
                  const3(dmat), const3(hmask), const3(qm), const3(km), const3(gc_t)],
        out_specs=pl.BlockSpec((C, RET_W), lambda b, r: (b * nr + r, 0)),
        out_shape=jax.ShapeDtypeStruct((m, RET_W), F32),
        scratch_shapes=[pltpu.VMEM((RET_HEADS, LANES, RET_DV), F32)],
        compiler_params=_cparams(("parallel", "arbitrary")),
        name="retention",
    )(zr, zr, zr, zr, cos_t, sin_t, dmat, hmask, qm, km, gc_t)


def _outproj_kernel(oh_ref, ys_ref, zu_ref, or_ref, d_ref, wg_ref, bg_ref, beta_ref, w_ref, x_ref, g_ref,
                    out_ref):
    beta = beta_ref[...]
    y = ys_ref[...] + d_ref[...] * zu_ref[...]
    y = 0.5 * y * (1.0 + lax.erf(y * (2.0 ** -0.5)))
    glu = jnp.dot(y.astype(BF16), wg_ref[...], preferred_element_type=F32) + bg_ref[...]
    o_s5 = y * jax.nn.sigmoid(glu)
    acc = jnp.dot((oh_ref[...] * beta[:, :HG_W]).astype(BF16), w_ref[:HG_W, :], preferred_element_type=F32)
    acc += jnp.dot((o_s5 * beta[:, HG_W:HG_W + S5_W]).astype(BF16), w_ref[HG_W:HG_W + S5_W, :],
                   preferred_element_type=F32)
    acc += jnp.dot((or_ref[...] * beta[:, HG_W + S5_W:]).astype(BF16), w_ref[HG_W + S5_W:, :],
                   preferred_element_type=F32)
    out_ref[...] = x_ref[...] + _rms_f32(acc) * g_ref[...]


def _outproj(o_hg, y_s5, zu, o_rt, d_skip, w_glu_bf, b_glu, beta, w_bf, x2, gain):
    m = x2.shape[0]
    row = lambda w: pl.BlockSpec((TM_OUT, w), lambda i: (i, 0))
    const = lambda r, c: pl.BlockSpec((r, c), lambda i: (0, 0))
    return pl.pallas_call(
        _outproj_kernel,
        grid=(m // TM_OUT,),
        in_specs=[row(HG_W), row(S5_W), row(S5_W), row(RET_W),
                  const(1, S5_W), const(S5_W, S5_W), const(1, S5_W), const(1, D_MIX),
                  const(D_MIX, D_MODEL), row(D_MODEL), const(1, D_MODEL)],
        out_specs=row(D_MODEL),
        out_shape=jax.ShapeDtypeStruct((m, D_MODEL), F32),
        compiler_params=_cparams(("parallel",)),
        name="outproj",
    )(o_hg, y_s5, zu, o_rt, d_skip, w_glu_bf, b_glu, beta, w_bf, x2, gain)


def _ffn_kernel(x_ref, xh_ref, gpre_ref, wg_ref, wv_ref, cwg_ref, cwv_ref, cbg_ref, cbv_ref, wd_ref,
                gpost_ref, out_ref, h_ref, acc_ref, *, tiles_per_seq):
    i = pl.program_id(0)
    j = pl.program_id(1)
    tm = TM_FFN

    @pl.when(j == 0)
    def _():
        gpre = gpre_ref[...]
        h_ref[FFN_HALO:, :] = (_rms_f32(x_ref[...]) * gpre).astype(BF16)
        halo = (_rms_f32(xh_ref[...]) * gpre).astype(BF16)
        first = (i % tiles_per_seq) == 0
        h_ref[:FFN_HALO, :] = jnp.where(first, jnp.zeros_like(halo), halo)
        acc_ref[...] = jnp.zeros_like(acc_ref)

    h = h_ref[...]

    def conv(w_ref, cw_ref, cb_ref):
        u = jnp.dot(h, w_ref[...], preferred_element_type=F32)
        cw = cw_ref[...]
        return (cw[0:1] * u[FFN_HALO - 2:FFN_HALO - 2 + tm] + cw[1:2] * u[FFN_HALO - 1:FFN_HALO - 1 + tm]
                + cw[2:3] * u[FFN_HALO:] + cb_ref[...])

    gate = conv(wg_ref, cwg_ref, cbg_ref)
    val = conv(wv_ref, cwv_ref, cbv_ref)
    act = (gate * jax.nn.sigmoid(gate) * val).astype(BF16)
    acc_ref[...] += jnp.dot(act, wd_ref[...], preferred_element_type=F32)

    @pl.when(j == pl.num_programs(1) - 1)
    def _():
        out_ref[...] = x_ref[...] + _rms_f32(acc_ref[...]) * gpost_ref[...]


def _ffn(x2, gpre, w_up_bf, conv_w, conv_b, w_down_bf, gpost, seq):
    m = x2.shape[0]
    tm, tn = TM_FFN, TN_FFN
    nj = D_FF // tn
    hb = tm // FFN_HALO
    const = lambda r, c: pl.BlockSpec((r, c), lambda i, j: (0, 0))
    return pl.pallas_call(
        functools.partial(_ffn_kernel, tiles_per_seq=seq // tm),
        grid=(m // tm, nj),
        in_specs=[pl.BlockSpec((tm, D_MODEL), lambda i, j: (i, 0)),
                  pl.BlockSpec((FFN_HALO, D_MODEL), lambda i, j: (jnp.maximum(i * hb - 1, 0), 0)),
                  const(1, D_MODEL),
                  pl.BlockSpec((D_MODEL, tn), lambda i, j: (0, j)),
                  pl.BlockSpec((D_MODEL, tn), lambda i, j: (0, nj + j)),
                  pl.BlockSpec((3, tn), lambda i, j: (0, j)),
                  pl.BlockSpec((3, tn), lambda i, j: (0, nj + j)),
                  pl.BlockSpec((1, tn), lambda i, j: (0, j)),
                  pl.BlockSpec((1, tn), lambda i, j: (0, nj + j)),
                  pl.BlockSpec((tn, D_MODEL), lambda i, j: (j, 0)),
                  const(1, D_MODEL)],
        out_specs=pl.BlockSpec((tm, D_MODEL), lambda i, j: (i, 0)),
        out_shape=jax.ShapeDtypeStruct((m, D_MODEL), F32),
        scratch_shapes=[pltpu.VMEM((tm + FFN_HALO, D_MODEL), BF16),
                        pltpu.VMEM((tm, D_MODEL), F32)],
        compiler_params=_cparams(("parallel", "arbitrary")),
        name="ffn",
    )(x2, x2, gpre, w_up_bf, w_up_bf, conv_w, conv_w, conv_b, conv_b, w_down_bf, gpost)


def kernel(x, w_in, w_out, mix_beta, hg_lb_logits, hg_onorm, s5_lam_re, s5_lam_im, s5_log_dt, s5_b_re,
           s5_b_im, s5_c_re, s5_c_im, s5_d, s5_w_glu, s5_b_glu, ffn_w_up, ffn_conv_w, ffn_conv_b,
           ffn_w_down, g_pre_mix, g_post_mix, g_pre_ffn, g_post_ffn):
    batch, seq, d = x.shape
    depth = w_in.shape[0]
    assert d == D_MODEL and seq % (S5_R * S5_L) == 0 and seq % TM_FFN == 0
    p = jax.nn.softmax(hg_lb_logits.astype(F32), axis=0)
    cum = jnp.cumsum(p, axis=0)
    lower_bounds = cum - cum[0:1]
    ret_tables = _ret_tables(seq)
    row = lambda a: a.astype(F32).reshape(1, -1)

    x2 = x.reshape(batch * seq, d)
    for l in range(depth):
        zh, zu, zr = _inproj(x2, row(g_pre_mix[l]), w_in[l].astype(BF16))
        o_hg = _hgrn(zh, row(lower_bounds[l]), row(hg_onorm[l]), batch, seq)
        s5_tabs = _s5_tables(s5_lam_re[l], s5_lam_im[l], s5_log_dt[l], s5_b_re[l], s5_b_im[l],
                             s5_c_re[l], s5_c_im[l])
        y_s5 = _s5(zu, s5_tabs, batch, seq)
        o_rt = _ret(zr, ret_tables, batch, seq)
        x2 = _outproj(o_hg, y_s5, zu, o_rt, row(s5_d[l]), s5_w_glu[l].astype(BF16), row(s5_b_glu[l]),
                      row(mix_beta[l]), w_out[l].astype(BF16), x2, row(g_post_mix[l]))
        x2 = _ffn(x2, row(g_pre_ffn[l]), ffn_w_up[l].astype(BF16), ffn_conv_w[l].astype(F32),
                  row(ffn_conv_b[l]), ffn_w_down[l].astype(BF16), row(g_post_ffn[l]), seq)
    return x2.reshape(batch, seq, d)
```

```python
import functools
import math

import jax
import jax.numpy as jnp
from jax import lax
from jax.experimental import pallas as pl
from jax.experimental.pallas import tpu as pltpu

F32 = jnp.float32
BF16 = jnp.bfloat16

D_MODEL = 2048
CHUNK = 64
HG_HEADS = 6
HG_D = 128
HG_W = HG_HEADS * HG_D
S5_GROUP_CH = 16
S5_W = 512
S5_GROUPS = 32
S5_STATE = 64
RET_HEADS = 6
RET_DK = 64
RET_DV = 128
RET_W = RET_HEADS * RET_DV
RET_QK = RET_HEADS * RET_DK
D_MIX = HG_W + S5_W + RET_W
ZH_W = 4 * HG_W
ZR_W = 2 * RET_QK + 2 * RET_W
N_IN = ZH_W + S5_W + ZR_W
D_FF = 5632
NORM_EPS = 1e-6
ROPE_BASE = 10000.0

LANES = 128
SUBL = 8
LOG2E = math.log2(math.e)
VMEM_LIMIT = 60000 * 1024

TM_IN = 256
TM_OUT = 512
TM_FFN = 512
TN_FFN = 512
FFN_HALO = 16
FFN_STRIP = 256
HG_ROWS = 512
HG_LEVELS = (32, 16, 8, 4, 2, 1)
HG_PAIR_W = 2 * HG_D
RET_CHUNK = 256
S5_L = 8
S5_R = 256
S5_OCT = 4
S5_OS = 8 * S5_STATE


def _cparams(sem):
    return pltpu.CompilerParams(dimension_semantics=sem, vmem_limit_bytes=VMEM_LIMIT)


def _rms_f32(x):
    return x * lax.rsqrt(jnp.mean(x * x, axis=-1, keepdims=True) + NORM_EPS)


def _inproj_kernel(x_ref, g_ref, w_ref, zh_ref, zu_ref, zr_ref):
    h = (_rms_f32(x_ref[...]) * g_ref[...]).astype(BF16)
    zh_ref[...] = jnp.dot(h, w_ref[:, :ZH_W], preferred_element_type=F32)
    zu_ref[...] = jnp.dot(h, w_ref[:, ZH_W:ZH_W + S5_W], preferred_element_type=F32)
    zr_ref[...] = jnp.dot(h, w_ref[:, ZH_W + S5_W:], preferred_element_type=F32)


def _inproj(x2, gain, w_bf, l):
    m = x2.shape[0]
    return pl.pallas_call(
        _inproj_kernel,
        grid=(m // TM_IN,),
        in_specs=[
            pl.BlockSpec((TM_IN, D_MODEL), lambda i: (i, 0)),
            pl.BlockSpec((1, D_MODEL), lambda i: (0, 0)),
            pl.BlockSpec((None, D_MODEL, N_IN), lambda i: (l, 0, 0), pipeline_mode=pl.Buffered(1)),
        ],
        out_specs=[
            pl.BlockSpec((TM_IN, ZH_W), lambda i: (i, 0)),
            pl.BlockSpec((TM_IN, S5_W), lambda i: (i, 0)),
            pl.BlockSpec((TM_IN, ZR_W), lambda i: (i, 0)),
        ],
        out_shape=[
            jax.ShapeDtypeStruct((m, ZH_W), F32),
            jax.ShapeDtypeStruct((m, S5_W), F32),
            jax.ShapeDtypeStruct((m, ZR_W), F32),
        ],
        compiler_params=_cparams(("parallel",)),
        name="inproj",
    )(x2, gain, w_bf)


def _hgrn_level_exponents(log_f):
    nv = CHUNK // SUBL
    w = log_f.shape[-1]
    x = log_f.reshape(nv, SUBL, w)
    sub = lax.broadcasted_iota(jnp.int32, (nv, SUBL, w), 1)
    loc = x
    for d in (1, 2, 4):
        loc = loc + jnp.where(sub >= d, pltpu.roll(loc, d, 1), 0.0)
    bc = lambda a, r: jnp.broadcast_to(a[:, r:r + 1, :], a.shape)
    tot = bc(loc, SUBL - 1)
    suf = tot - loc
    tots = [tot[j] for j in range(nv)]
    locs = [loc[j] for j in range(nv)]
    sufs = [suf[j] for j in range(nv)]

    def span(js):
        acc = None
        for j in js:
            acc = tots[j] if acc is None else acc + tots[j]
        return acc

    def plus(a, s):
        return a if s is None else a + s

    b = jnp.stack([plus(locs[j], span(range(j))) for j in range(nv)])
    b_rev = jnp.stack([plus(sufs[j], span(range(j + 1, nv))) for j in range(nv)])
    levels = []
    for h in HG_LEVELS:
        if h >= SUBL:
            g = h // SUBL
            rows = []
            for j in range(nv):
                blk = j // g
                if blk % 2 == 1:
                    rows.append(plus(locs[j], span(range(blk * g, j))))
                else:
                    rows.append(plus(sufs[j], span(range(j + 1, (blk + 1) * g))))
            levels.append(jnp.stack(rows))
        elif h == 4:
            ref = bc(loc, 3)
            levels.append(jnp.where(sub >= 4, loc - ref, ref - loc))
        elif h == 2:
            ref = jnp.where(sub < 4, bc(loc, 1), bc(loc, 5))
            levels.append(jnp.where((sub & 3) >= 2, loc - ref, ref - loc))
        else:
            levels.append(jnp.where((sub & 1) == 1, x, 0.0))
    flat = lambda a: a.reshape(CHUNK, w)
    return flat(b), flat(b_rev), [flat(l) for l in levels]


def _hgrn_kernel(q_ref, f_ref, i_ref, g_ref, lb_ref, on_ref, o_ref, st_ref):
    @pl.when(pl.program_id(2) == 0)
    def _():
        st_ref[...] = jnp.zeros_like(st_ref)

    lb = lb_ref[...]
    log_lb = jnp.log(lb)
    log_1mlb = jnp.log1p(-lb)
    one_mlb = 1.0 - lb
    onorm = on_ref[...]
    t2 = lax.broadcasted_iota(jnp.int32, (CHUNK, HG_D), 0)
    s2 = lax.broadcasted_iota(jnp.int32, (CHUNK, HG_D), 1) & (CHUNK - 1)
    top_bit = t2 ^ s2
    level_masks = [(top_bit >= h) & (top_bit < 2 * h) & (t2 > s2) for h in HG_LEVELS]
    diag_mask = t2 == s2
    r = lax.broadcasted_iota(jnp.int32, (HG_PAIR_W, HG_PAIR_W), 0) >= HG_D
    c = lax.broadcasted_iota(jnp.int32, (HG_PAIR_W, HG_PAIR_W), 1) >= HG_D
    same_head = r == c
    nt = (((1,), (1,)), ((), ()))
    tn = (((0,), (0,)), ((), ()))
    zeros_bf = jnp.zeros((CHUNK, HG_D), BF16)

    def block_diag_rows(a):
        return jnp.concatenate([jnp.concatenate([a[:, :HG_D], zeros_bf], axis=1),
                                jnp.concatenate([zeros_bf, a[:, HG_D:]], axis=1)], axis=0)

    for ch in range(HG_ROWS // CHUNK):
        rows = pl.ds(ch * CHUNK, CHUNK)
        qr = q_ref[rows, :]
        fr = f_ref[rows, :]
        gr = g_ref[rows, :]
        v_bf = i_ref[rows, :].astype(BF16)
        q = qr * jax.nn.sigmoid(qr)
        e = jnp.exp(-jnp.abs(fr))
        log_sig = jnp.minimum(fr, 0.0) - jnp.log(1.0 + e)
        cc = log_1mlb + log_sig
        log_f = jnp.maximum(log_lb, cc) + jnp.log(1.0 + jnp.exp(-jnp.abs(log_lb - cc)))
        k = one_mlb * (jnp.where(fr >= 0.0, e, 1.0) / (1.0 + e))
        b, b_rev, level_exps = _hgrn_level_exponents(log_f * LOG2E)
        e_b = jnp.exp2(b)
        q_bf = q.astype(BF16)
        k_bf = k.astype(BF16)
        st = st_ref[...]
        o = lax.dot_general((q * e_b).astype(BF16), st.astype(BF16), nt, preferred_element_type=F32)
        k_dec = (k * jnp.exp2(b_rev)).astype(BF16)
        kv = lax.dot_general(v_bf, k_dec, tn, preferred_element_type=F32)
        st_ref[...] = st * e_b[CHUNK - 1:CHUNK, :] + jnp.where(same_head, kv, 0.0)
        a = jnp.where(diag_mask, lax.dot_general(q_bf, block_diag_rows(k_bf), nt,
                                                 preferred_element_type=F32), 0.0)
        for d_h, mask in zip(level_exps, level_masks):
            e_h = jnp.exp2(d_h).astype(BF16)
            p = lax.dot_general(q_bf * e_h, block_diag_rows(k_bf * e_h), nt, preferred_element_type=F32)
            a = jnp.where(mask, p, a)
        o = o + jnp.dot(a.astype(BF16), block_diag_rows(v_bf), preferred_element_type=F32)
        gate = gr * jax.nn.sigmoid(gr)
        for hh in range(2):
            cols = slice(hh * HG_D, (hh + 1) * HG_D)
            o_ref[rows, cols] = _rms_f32(o[:, cols]) * onorm[:, cols] * gate[:, cols]


def _hgrn(zh, lb, onorm, batch, seq):
    m = zh.shape[0]
    nr = seq // HG_ROWS
    npair = HG_HEADS // 2
    onorm2 = jnp.concatenate([onorm, onorm], axis=1)

    def col(kk):
        return pl.BlockSpec((HG_ROWS, HG_PAIR_W), lambda b, h, r: (b * nr + r, kk * npair + h))

    return pl.pallas_call(
        _hgrn_kernel,
        grid=(batch, npair, nr),
        in_specs=[col(0), col(1), col(2), col(3),
                  pl.BlockSpec((1, HG_PAIR_W), lambda b, h, r: (0, h)),
                  pl.BlockSpec((1, HG_PAIR_W), lambda b, h, r: (0, 0))],
        out_specs=pl.BlockSpec((HG_ROWS, HG_PAIR_W), lambda b, h, r: (b * nr + r, h)),
        out_shape=jax.ShapeDtypeStruct((m, HG_W), F32),
        scratch_shapes=[pltpu.VMEM((HG_PAIR_W, HG_PAIR_W), F32)],
        compiler_params=_cparams(("parallel", "parallel", "arbitrary")),
        name="hgrn2",
    )(zh, zh, zh, zh, lb, onorm2)


def _s5_tables(lam_re, lam_im, log_dt, b_re, b_im, c_re, c_im):
    hp = lax.Precision.HIGHEST
    L = S5_L
    dt = jnp.exp(log_dt.astype(F32))[:, None]
    lr = jnp.minimum(lam_re.astype(F32), -1e-4)
    li = lam_im.astype(F32)
    mag = jnp.exp(lr * dt)
    ar, ai = mag * jnp.cos(li * dt), mag * jnp.sin(li * dt)
    den = lr * lr + li * li
    nr_, ni_ = ar - 1.0, ai
    cr, ci = (nr_ * lr + ni_ * li) / den, (ni_ * lr - nr_ * li) / den
    br_, bi_ = b_re.astype(F32), b_im.astype(F32)
    bbr = cr[..., None] * br_ - ci[..., None] * bi_
    bbi = cr[..., None] * bi_ + ci[..., None] * br_
    cre, cim = c_re.astype(F32), c_im.astype(F32)
    jj = jnp.arange(L + 1, dtype=F32)[:, None, None]
    pmag = jnp.exp(lr * dt * jj)
    pr, pi = pmag * jnp.cos(li * dt * jj), pmag * jnp.sin(li * dt * jj)

    G, P, CH = S5_GROUPS, S5_STATE, S5_GROUP_CH
    ein = functools.partial(jnp.einsum, precision=hp)
    kj = (ein('gcp,jgp,gpd->jgcd', cre, pr[:L], bbr) - ein('gcp,jgp,gpd->jgcd', cre, pi[:L], bbi)
          - ein('gcp,jgp,gpd->jgcd', cim, pr[:L], bbi) - ein('gcp,jgp,gpd->jgcd', cim, pi[:L], bbr))
    eye8 = jnp.eye(8, dtype=F32)
    kb = jnp.einsum('joicd,ik->joidkc', kj.reshape(L, S5_OCT, 8, CH, CH), eye8)
    kb = kb.reshape(L, S5_OCT, LANES, LANES)
    s_idx = jnp.arange(L)[:, None]
    t_idx = jnp.arange(L)[None, :]
    lag = t_idx - s_idx
    toe = jnp.where((lag >= 0)[:, :, None, None, None], kb[jnp.clip(lag, 0, L - 1)], 0.0)
    w_toe = toe.transpose(2, 0, 3, 1, 4).reshape(S5_OCT, L * LANES, L * LANES)

    prs, pis = pr[L - 1 - jnp.arange(L)], pi[L - 1 - jnp.arange(L)]
    vre = prs[..., None] * bbr[None] - pis[..., None] * bbi[None]
    vim = prs[..., None] * bbi[None] + pis[..., None] * bbr[None]

    def v_lay(z):
        z = z.reshape(L, S5_OCT, 8, P, CH)
        z = jnp.einsum('soipd,ik->soidkp', z, eye8)
        return z.reshape(L, S5_OCT, LANES, S5_OS).transpose(1, 0, 2, 3).reshape(S5_OCT, L * LANES, S5_OS)

    w_in = jnp.concatenate([v_lay(vre), v_lay(vim)], axis=-1)

    prt, pit = pr[1:L + 1], pi[1:L + 1]
    mre = cre[None] * prt[:, :, None, :] - cim[None] * pit[:, :, None, :]
    mim = -(cre[None] * pit[:, :, None, :] + cim[None] * prt[:, :, None, :])

    def m_lay(z):
        z = z.reshape(L, S5_OCT, 8, CH, P)
        z = jnp.einsum('toicp,ik->toipkc', z, eye8)
        return z.reshape(L, S5_OCT, S5_OS, LANES).transpose(1, 2, 0, 3).reshape(S5_OCT, S5_OS, L * LANES)

    w_out = jnp.concatenate([m_lay(mre), m_lay(mim)], axis=1)
    a_l = jnp.concatenate([pr[L].reshape(S5_OCT, 1, S5_OS), pi[L].reshape(S5_OCT, 1, S5_OS)], axis=-1)
    return w_toe.astype(BF16), w_in.astype(BF16), w_out.astype(BF16), a_l


def _s5_kernel(u_ref, wt_ref, wi_ref, wo_ref, al_ref, y_ref, s_ref, v_ref, sp_ref):
    @pl.when(pl.program_id(2) == 0)
    def _():
        s_ref[...] = jnp.zeros_like(s_ref)

    L, R = S5_L, S5_R
    lhs = jnp.concatenate([u_ref[pl.ds(s, R, stride=L), :] for s in range(L)], axis=1).astype(BF16)
    y_intra = jnp.dot(lhs, wt_ref[0], preferred_element_type=F32)
    v_ref[...] = jnp.dot(lhs, wi_ref[0], preferred_element_type=F32)
    ar = al_ref[0, :, :S5_OS]
    ai = al_ref[0, :, S5_OS:]

    def step(n, carry):
        sr, si = carry
        sp_ref[pl.ds(n, 1), :S5_OS] = sr
        sp_ref[pl.ds(n, 1), S5_OS:] = si
        vr = v_ref[pl.ds(n, 1), :S5_OS]
        vi = v_ref[pl.ds(n, 1), S5_OS:]
        return (ar * sr - ai * si + vr, ar * si + ai * sr + vi)

    sr, si = lax.fori_loop(0, R, step, (s_ref[:, :S5_OS], s_ref[:, S5_OS:]), unroll=8)
    s_ref[:, :S5_OS] = sr
    s_ref[:, S5_OS:] = si
    y = y_intra + jnp.dot(sp_ref[...].astype(BF16), wo_ref[0], preferred_element_type=F32)
    for t in range(L):
        y_ref[pl.ds(t, R, stride=L), :] = y[:, t * LANES:(t + 1) * LANES]


def _s5(zu, tables, batch, seq):
    w_toe, w_in, w_out, a_l = tables
    m = zu.shape[0]
    rows = S5_R * S5_L
    nr = seq // rows
    full = lambda shp: pl.BlockSpec((1,) + shp, lambda o, b, r: (o, 0, 0))
    return pl.pallas_call(
        _s5_kernel,
        grid=(S5_OCT, batch, nr),
        in_specs=[pl.BlockSpec((rows, LANES), lambda o, b, r: (b * nr + r, o)),
                  full((S5_L * LANES, S5_L * LANES)), full((S5_L * LANES, 2 * S5_OS)),
                  full((2 * S5_OS, S5_L * LANES)), full((1, 2 * S5_OS))],
        out_specs=pl.BlockSpec((rows, LANES), lambda o, b, r: (b * nr + r, o)),
        out_shape=jax.ShapeDtypeStruct((m, S5_W), F32),
        scratch_shapes=[pltpu.VMEM((1, 2 * S5_OS), F32),
                        pltpu.VMEM((S5_R, 2 * S5_OS), F32),
                        pltpu.VMEM((S5_R, 2 * S5_OS), F32)],
        compiler_params=_cparams(("parallel", "parallel", "arbitrary")),
        name="s5",
    )(zu, w_toe, w_in, w_out, a_l)


def _ret_tables(seq):
    C = RET_CHUNK
    pos = jnp.arange(seq, dtype=F32)
    inv_freq = 1.0 / (ROPE_BASE ** jnp.linspace(0.0, 1.0, RET_DK // 2, dtype=F32))
    ang = pos[:, None] * inv_freq[None, :]
    cos, sin = jnp.cos(ang), jnp.sin(ang)
    cos_t = jnp.concatenate([cos, cos, cos, cos], axis=1)
    sin_t = jnp.concatenate([-sin, sin, -sin, sin], axis=1)
    log_g = jnp.log(1.0 - 2.0 ** (-5.0 - jnp.arange(RET_HEADS, dtype=F32)))
    idx = jnp.arange(C, dtype=F32)
    rel = idx[:, None] - idx[None, :]
    dmat = jnp.where(rel[None] >= 0, jnp.exp(jnp.maximum(rel, 0.0)[None] * log_g[:, None, None]), 0.0)
    lane = jnp.arange(LANES)
    head_mask = ((lane[None, :] // RET_DK) == (jnp.arange(RET_HEADS) % 2)[:, None]).astype(F32)
    gq = jnp.exp((idx + 1.0)[None, :] * log_g[:, None])
    gk = jnp.exp((C - 1.0 - idx)[None, :] * log_g[:, None])
    qm = gq[:, :, None] * head_mask[:, None, :]
    km = gk[:, :, None] * head_mask[:, None, :]
    gc = jnp.exp(C * log_g)
    gc_t = jnp.broadcast_to(gc[:, None, None], (RET_HEADS, 1, LANES))
    return cos_t, sin_t, dmat, head_mask[:, None, :], qm, km, gc_t


def _ret_kernel(q_ref, k_ref, v_ref, g_ref, cos_ref, sin_ref, dm_ref, hm_ref, qm_ref, km_ref, gc_ref,
                o_ref, st_ref):
    @pl.when(pl.program_id(1) == 0)
    def _():
        st_ref[...] = jnp.zeros_like(st_ref)

    cos_t = cos_ref[...]
    sin_t = sin_ref[...]
    lane = lax.broadcasted_iota(jnp.int32, (RET_CHUNK, LANES), 1)
    first_half = (lane % RET_DK) < (RET_DK // 2)
    nt = (((1,), (1,)), ((), ()))
    tn = (((0,), (0,)), ((), ()))

    def rope(z):
        swapped = jnp.where(first_half, pltpu.roll(z, LANES - RET_DK // 2, 1), pltpu.roll(z, RET_DK // 2, 1))
        return z * cos_t + swapped * sin_t

    for tile in range(RET_QK // LANES):
        cols = slice(tile * LANES, (tile + 1) * LANES)
        q_t = rope(q_ref[:, cols])
        k_t = rope(k_ref[:, cols]) * (RET_DK ** -0.5)
        k_bf = k_t.astype(BF16)
        for half in range(2):
            h = 2 * tile + half
            vcols = slice(h * RET_DV, (h + 1) * RET_DV)
            v_bf = v_ref[:, vcols].astype(BF16)
            q_h = (q_t * hm_ref[h]).astype(BF16)
            scores = lax.dot_general(q_h, k_bf, nt, preferred_element_type=F32) * dm_ref[h]
            o = jnp.dot(scores.astype(BF16), v_bf, preferred_element_type=F32)
            st = st_ref[h]
            o = o + jnp.dot((q_t * qm_ref[h]).astype(BF16), st.astype(BF16), preferred_element_type=F32)
            k_dec = (k_t * km_ref[h]).astype(BF16)
            st_ref[h] = st * gc_ref[h] + lax.dot_general(k_dec, v_bf, tn, preferred_element_type=F32)
            gr = g_ref[:, vcols]
            o_ref[:, vcols] = _rms_f32(o) * (gr * jax.nn.sigmoid(gr))


def _ret(zr, tables, batch, seq):
    cos_t, sin_t, dmat, hmask, qm, km, gc_t = tables
    m = zr.shape[0]
    C = RET_CHUNK
    nr = seq // C
    const3 = lambda a: pl.BlockSpec(a.shape, lambda b, r: (0, 0, 0))
    return pl.pallas_call(
        _ret_kernel,
        grid=(batch, nr),
        in_specs=[pl.BlockSpec((C, RET_QK), lambda b, r: (b * nr + r, 0)),
                  pl.BlockSpec((C, RET_QK), lambda b, r: (b * nr + r, 1)),
                  pl.BlockSpec((C, RET_W), lambda b, r: (b * nr + r, 1)),
                  pl.BlockSpec((C, RET_W), lambda b, r: (b * nr + r, 2)),
                  pl.BlockSpec((C, LANES), lambda b, r: (r, 0)),
                  pl.BlockSpec((C, LANES), lambda b, r: (r, 0)),
                  const3(dmat), const3(hmask), const3(qm), const3(km), const3(gc_t)],
        out_specs=pl.BlockSpec((C, RET_W), lambda b, r: (b * nr + r, 0)),
        out_shape=jax.ShapeDtypeStruct((m, RET_W), F32),
        scratch_shapes=[pltpu.VMEM((RET_HEADS, LANES, RET_DV), F32)],
        compiler_params=_cparams(("parallel", "arbitrary")),
        name="retention",
    )(zr, zr, zr, zr, cos_t, sin_t, dmat, hmask, qm, km, gc_t)


def _outproj_kernel(oh_ref, ys_ref, zu_ref, or_ref, d_ref, wg_ref, bg_ref, beta_ref, w_ref, x_ref, g_ref,
                    out_ref):
    beta = beta_ref[...]
    y = ys_ref[...] + d_ref[...] * zu_ref[...]
    y = 0.5 * y * (1.0 + lax.erf(y * (2.0 ** -0.5)))
    glu = jnp.dot(y.astype(BF16), wg_ref[...], preferred_element_type=F32) + bg_ref[...]
    o_s5 = y * jax.nn.sigmoid(glu)
    acc = jnp.dot((oh_ref[...] * beta[:, :HG_W]).astype(BF16), w_ref[:HG_W, :], preferred_element_type=F32)
    acc += jnp.dot((o_s5 * beta[:, HG_W:HG_W + S5_W]).astype(BF16), w_ref[HG_W:HG_W + S5_W, :],
                   preferred_element_type=F32)
    acc += jnp.dot((or_ref[...] * beta[:, HG_W + S5_W:]).astype(BF16), w_ref[HG_W + S5_W:, :],
                   preferred_element_type=F32)
    out_ref[...] = x_ref[...] + _rms_f32(acc) * g_ref[...]


def _outproj(o_hg, y_s5, zu, o_rt, d_skip, w_glu_bf, b_glu, beta, w_bf, x2, gain, l):
    m = x2.shape[0]
    row = lambda w: pl.BlockSpec((TM_OUT, w), lambda i: (i, 0))
    const = lambda r, c: pl.BlockSpec((r, c), lambda i: (0, 0))
    layer = lambda r, c: pl.BlockSpec((None, r, c), lambda i: (l, 0, 0))
    return pl.pallas_call(
        _outproj_kernel,
        grid=(m // TM_OUT,),
        in_specs=[row(HG_W), row(S5_W), row(S5_W), row(RET_W),
                  const(1, S5_W), layer(S5_W, S5_W), const(1, S5_W), const(1, D_MIX),
                  layer(D_MIX, D_MODEL), row(D_MODEL), const(1, D_MODEL)],
        out_specs=row(D_MODEL),
        out_shape=jax.ShapeDtypeStruct((m, D_MODEL), F32),
        compiler_params=_cparams(("parallel",)),
        name="outproj",
    )(o_hg, y_s5, zu, o_rt, d_skip, w_glu_bf, b_glu, beta, w_bf, x2, gain)


def _ffn_kernel(x_ref, xh_ref, gpre_ref, wg_ref, wv_ref, cwg_ref, cwv_ref, cbg_ref, cbv_ref, wd_ref,
                gpost_ref, out_ref, h_ref, ug_ref, uv_ref, *, tiles_per_seq):
    i = pl.program_id(0)
    j = pl.program_id(1)

    @pl.when(j == 0)
    def _():
        gpre = gpre_ref[...]
        h_ref[FFN_HALO:, :] = (_rms_f32(x_ref[...]) * gpre).astype(BF16)
        halo = (_rms_f32(xh_ref[...]) * gpre).astype(BF16)
        first = (i % tiles_per_seq) == 0
        h_ref[:FFN_HALO, :] = jnp.where(first, jnp.zeros_like(halo), halo)
        out_ref[...] = jnp.zeros_like(out_ref)

    h = h_ref[...]
    ug_ref[...] = jnp.dot(h, wg_ref[...], preferred_element_type=F32)
    uv_ref[...] = jnp.dot(h, wv_ref[...], preferred_element_type=F32)
    cwg, cwv, cbg, cbv = cwg_ref[...], cwv_ref[...], cbg_ref[...], cbv_ref[...]

    def conv(u_ref, cw, cb, r0):
        taps = [u_ref[pl.ds(FFN_HALO - 2 + tap + r0, FFN_STRIP), :] for tap in range(3)]
        return cw[0:1] * taps[0] + cw[1:2] * taps[1] + cw[2:3] * taps[2] + cb

    for s in range(TM_FFN // FFN_STRIP):
        r0 = s * FFN_STRIP
        gate = conv(ug_ref, cwg, cbg, r0)
        val = conv(uv_ref, cwv, cbv, r0)
        act = (gate * jax.nn.sigmoid(gate) * val).astype(BF16)
        out_ref[pl.ds(r0, FFN_STRIP), :] += jnp.dot(act, wd_ref[...], preferred_element_type=F32)

    @pl.when(j == pl.num_programs(1) - 1)
    def _():
        out_ref[...] = x_ref[...] + _rms_f32(out_ref[...]) * gpost_ref[...]


def _ffn(x2, gpre, w_up_bf, conv_w, conv_b, w_down_bf, gpost, seq, l):
    m = x2.shape[0]
    tm, tn = TM_FFN, TN_FFN
    nj = D_FF // tn
    hb = tm // FFN_HALO
    const = lambda r, c: pl.BlockSpec((r, c), lambda i, j: (0, 0))
    return pl.pallas_call(
        functools.partial(_ffn_kernel, tiles_per_seq=seq // tm),
        grid=(m // tm, nj),
        in_specs=[pl.BlockSpec((tm, D_MODEL), lambda i, j: (i, 0)),
                  pl.BlockSpec((FFN_HALO, D_MODEL), lambda i, j: (jnp.maximum(i * hb - 1, 0), 0)),
                  const(1, D_MODEL),
                  pl.BlockSpec((None, D_MODEL, tn), lambda i, j: (l, 0, j)),
                  pl.BlockSpec((None, D_MODEL, tn), lambda i, j: (l, 0, nj + j)),
                  pl.BlockSpec((3, tn), lambda i, j: (0, j)),
                  pl.BlockSpec((3, tn), lambda i, j: (0, nj + j)),
                  pl.BlockSpec((1, tn), lambda i, j: (0, j)),
                  pl.BlockSpec((1, tn), lambda i, j: (0, nj + j)),
                  pl.BlockSpec((None, tn, D_MODEL), lambda i, j: (l, j, 0)),
                  const(1, D_MODEL)],
        out_specs=pl.BlockSpec((tm, D_MODEL), lambda i, j: (i, 0)),
        out_shape=jax.ShapeDtypeStruct((m, D_MODEL), F32),
        scratch_shapes=[pltpu.VMEM((tm + FFN_HALO, D_MODEL), BF16),
                        pltpu.VMEM((tm + FFN_HALO, tn), F32),
                        pltpu.VMEM((tm + FFN_HALO, tn), F32)],
        compiler_params=_cparams(("parallel", "arbitrary")),
        name="ffn",
    )(x2, x2, gpre, w_up_bf, w_up_bf, conv_w, conv_w, conv_b, conv_b, w_down_bf, gpost)


def kernel(x, w_in, w_out, mix_beta, hg_lb_logits, hg_onorm, s5_lam_re, s5_lam_im, s5_log_dt, s5_b_re,
           s5_b_im, s5_c_re, s5_c_im, s5_d, s5_w_glu, s5_b_glu, ffn_w_up, ffn_conv_w, ffn_conv_b,
           ffn_w_down, g_pre_mix, g_post_mix, g_pre_ffn, g_post_ffn):
    batch, seq, d = x.shape
    depth = w_in.shape[0]
    assert d == D_MODEL and seq % (S5_R * S5_L) == 0 and seq % TM_FFN == 0
    p = jax.nn.softmax(hg_lb_logits.astype(F32), axis=0)
    cum = jnp.cumsum(p, axis=0)
    lower_bounds = cum - cum[0:1]
    ret_tables = _ret_tables(seq)
    row = lambda a: a.astype(F32).reshape(1, -1)

    w_in_bf, w_out_bf, w_glu_bf = w_in.astype(BF16), w_out.astype(BF16), s5_w_glu.astype(BF16)
    w_up_bf, w_down_bf = ffn_w_up.astype(BF16), ffn_w_down.astype(BF16)
    x2 = x.reshape(batch * seq, d)
    for l in range(depth):
        zh, zu, zr = _inproj(x2, row(g_pre_mix[l]), w_in_bf, l)
        o_hg = _hgrn(zh, row(lower_bounds[l]), row(hg_onorm[l]), batch, seq)
        s5_tabs = _s5_tables(s5_lam_re[l], s5_lam_im[l], s5_log_dt[l], s5_b_re[l], s5_b_im[l],
                             s5_c_re[l], s5_c_im[l])
        y_s5 = _s5(zu, s5_tabs, batch, seq)
        o_rt = _ret(zr, ret_tables, batch, seq)
        x2 = _outproj(o_hg, y_s5, zu, o_rt, row(s5_d[l]), w_glu_bf, row(s5_b_glu[l]),
                      row(mix_beta[l]), w_out_bf, x2, row(g_post_mix[l]), l)
        x2 = _ffn(x2, row(g_pre_ffn[l]), w_up_bf, ffn_conv_w[l].astype(F32),
                  row(ffn_conv_b[l]), w_down_bf, row(g_post_ffn[l]), seq, l)
    return x2.reshape(batch, seq, d)
```

```python
import functools
import math

import jax
import jax.numpy as jnp
from jax import lax
from jax.experimental import pallas as pl
from jax.experimental.pallas import tpu as pltpu

F32 = jnp.float32
BF16 = jnp.bfloat16

D_MODEL = 2048
CHUNK = 64
HG_HEADS = 6
HG_D = 128
HG_W = HG_HEADS * HG_D
S5_GROUP_CH = 16
S5_W = 512
S5_GROUPS = 32
S5_STATE = 64
RET_HEADS = 6
RET_DK = 64
RET_DV = 128
RET_W = RET_HEADS * RET_DV
RET_QK = RET_HEADS * RET_DK
D_MIX = HG_W + S5_W + RET_W
ZH_W = 4 * HG_W
ZR_W = 2 * RET_QK + 2 * RET_W
N_IN = ZH_W + S5_W + ZR_W
D_FF = 5632
NORM_EPS = 1e-6
ROPE_BASE = 10000.0

LANES = 128
SUBL = 8
LOG2E = math.log2(math.e)
VMEM_LIMIT = 60000 * 1024

TM_IN = 256
TM_OUT = 512
TM_FFN = 512
TN_FFN = 512
FFN_HALO = 16
FFN_STRIP = 256
HG_LEVELS = (32, 16, 8, 4, 2, 1)
HG_PAIR_W = 2 * HG_D
RET_CHUNK = 256
S5_L = 8
S5_R = 256
S5_OCT = 4
S5_OS = 8 * S5_STATE


def _cparams(sem):
    return pltpu.CompilerParams(dimension_semantics=sem, vmem_limit_bytes=VMEM_LIMIT)


def _rms_f32(x):
    return x * lax.rsqrt(jnp.mean(x * x, axis=-1, keepdims=True) + NORM_EPS)


def _hgrn_level_exponents(log_f):
    nv = CHUNK // SUBL
    w = log_f.shape[-1]
    x = log_f.reshape(nv, SUBL, w)
    sub = lax.broadcasted_iota(jnp.int32, (nv, SUBL, w), 1)
    loc = x
    for d in (1, 2, 4):
        loc = loc + jnp.where(sub >= d, pltpu.roll(loc, d, 1), 0.0)
    bc = lambda a, r: jnp.broadcast_to(a[:, r:r + 1, :], a.shape)
    tot = bc(loc, SUBL - 1)
    suf = tot - loc
    tots = [tot[j] for j in range(nv)]
    locs = [loc[j] for j in range(nv)]
    sufs = [suf[j] for j in range(nv)]

    def span(js):
        acc = None
        for j in js:
            acc = tots[j] if acc is None else acc + tots[j]
        return acc

    def plus(a, s):
        return a if s is None else a + s

    b = jnp.stack([plus(locs[j], span(range(j))) for j in range(nv)])
    b_rev = jnp.stack([plus(sufs[j], span(range(j + 1, nv))) for j in range(nv)])
    levels = []
    for h in HG_LEVELS:
        if h >= SUBL:
            g = h // SUBL
            rows = []
            for j in range(nv):
                blk = j // g
                if blk % 2 == 1:
                    rows.append(plus(locs[j], span(range(blk * g, j))))
                else:
                    rows.append(plus(sufs[j], span(range(j + 1, (blk + 1) * g))))
            levels.append(jnp.stack(rows))
        elif h == 4:
            ref = bc(loc, 3)
            levels.append(jnp.where(sub >= 4, loc - ref, ref - loc))
        elif h == 2:
            ref = jnp.where(sub < 4, bc(loc, 1), bc(loc, 5))
            levels.append(jnp.where((sub & 3) >= 2, loc - ref, ref - loc))
        else:
            levels.append(jnp.where((sub & 1) == 1, x, 0.0))
    flat = lambda a: a.reshape(CHUNK, w)
    return flat(b), flat(b_rev), [flat(l) for l in levels]


def _hgrn_rows(q_ref, f_ref, i_ref, g_ref, lb, onorm, o_ref, st_ref, nrows):
    log_lb = jnp.log(lb)
    log_1mlb = jnp.log1p(-lb)
    one_mlb = 1.0 - lb
    t2 = lax.broadcasted_iota(jnp.int32, (CHUNK, HG_D), 0)
    s2 = lax.broadcasted_iota(jnp.int32, (CHUNK, HG_D), 1) & (CHUNK - 1)
    top_bit = t2 ^ s2
    level_masks = [(top_bit >= h) & (top_bit < 2 * h) & (t2 > s2) for h in HG_LEVELS]
    diag_mask = t2 == s2
    r = lax.broadcasted_iota(jnp.int32, (HG_PAIR_W, HG_PAIR_W), 0) >= HG_D
    c = lax.broadcasted_iota(jnp.int32, (HG_PAIR_W, HG_PAIR_W), 1) >= HG_D
    same_head = r == c
    nt = (((1,), (1,)), ((), ()))
    tn = (((0,), (0,)), ((), ()))
    zeros_bf = jnp.zeros((CHUNK, HG_D), BF16)

    def block_diag_rows(a):
        return jnp.concatenate([jnp.concatenate([a[:, :HG_D], zeros_bf], axis=1),
                                jnp.concatenate([zeros_bf, a[:, HG_D:]], axis=1)], axis=0)

    for ch in range(nrows // CHUNK):
        rows = pl.ds(ch * CHUNK, CHUNK)
        qr = q_ref[rows, :]
        fr = f_ref[rows, :]
        gr = g_ref[rows, :]
        v_bf = i_ref[rows, :].astype(BF16)
        q = qr * jax.nn.sigmoid(qr)
        e = jnp.exp(-jnp.abs(fr))
        log_sig = jnp.minimum(fr, 0.0) - jnp.log(1.0 + e)
        cc = log_1mlb + log_sig
        log_f = jnp.maximum(log_lb, cc) + jnp.log(1.0 + jnp.exp(-jnp.abs(log_lb - cc)))
        k = one_mlb * (jnp.where(fr >= 0.0, e, 1.0) / (1.0 + e))
        b, b_rev, level_exps = _hgrn_level_exponents(log_f * LOG2E)
        e_b = jnp.exp2(b)
        q_bf = q.astype(BF16)
        k_bf = k.astype(BF16)
        st = st_ref[...]
        o = lax.dot_general((q * e_b).astype(BF16), st.astype(BF16), nt, preferred_element_type=F32)
        k_dec = (k * jnp.exp2(b_rev)).astype(BF16)
        kv = lax.dot_general(v_bf, k_dec, tn, preferred_element_type=F32)
        st_ref[...] = st * e_b[CHUNK - 1:CHUNK, :] + jnp.where(same_head, kv, 0.0)
        a = jnp.where(diag_mask, lax.dot_general(q_bf, block_diag_rows(k_bf), nt,
                                                 preferred_element_type=F32), 0.0)
        for d_h, mask in zip(level_exps, level_masks):
            e_h = jnp.exp2(d_h).astype(BF16)
            p = lax.dot_general(q_bf * e_h, block_diag_rows(k_bf * e_h), nt, preferred_element_type=F32)
            a = jnp.where(mask, p, a)
        o = o + jnp.dot(a.astype(BF16), block_diag_rows(v_bf), preferred_element_type=F32)
        gate = gr * jax.nn.sigmoid(gr)
        for hh in range(2):
            cols = slice(hh * HG_D, (hh + 1) * HG_D)
            o_ref[rows, cols] = _rms_f32(o[:, cols]) * onorm[:, cols] * gate[:, cols]


def _s5_tables(lam_re, lam_im, log_dt, b_re, b_im, c_re, c_im):
    hp = lax.Precision.HIGHEST
    L = S5_L
    dt = jnp.exp(log_dt.astype(F32))[:, None]
    lr = jnp.minimum(lam_re.astype(F32), -1e-4)
    li = lam_im.astype(F32)
    mag = jnp.exp(lr * dt)
    ar, ai = mag * jnp.cos(li * dt), mag * jnp.sin(li * dt)
    den = lr * lr + li * li
    nr_, ni_ = ar - 1.0, ai
    cr, ci = (nr_ * lr + ni_ * li) / den, (ni_ * lr - nr_ * li) / den
    br_, bi_ = b_re.astype(F32), b_im.astype(F32)
    bbr = cr[..., None] * br_ - ci[..., None] * bi_
    bbi = cr[..., None] * bi_ + ci[..., None] * br_
    cre, cim = c_re.astype(F32), c_im.astype(F32)
    jj = jnp.arange(L + 1, dtype=F32)[:, None, None]
    pmag = jnp.exp(lr * dt * jj)
    pr, pi = pmag * jnp.cos(li * dt * jj), pmag * jnp.sin(li * dt * jj)

    G, P, CH = S5_GROUPS, S5_STATE, S5_GROUP_CH
    ein = functools.partial(jnp.einsum, precision=hp)
    kj = (ein('gcp,jgp,gpd->jgcd', cre, pr[:L], bbr) - ein('gcp,jgp,gpd->jgcd', cre, pi[:L], bbi)
          - ein('gcp,jgp,gpd->jgcd', cim, pr[:L], bbi) - ein('gcp,jgp,gpd->jgcd', cim, pi[:L], bbr))
    eye8 = jnp.eye(8, dtype=F32)
    kb = jnp.einsum('joicd,ik->joidkc', kj.reshape(L, S5_OCT, 8, CH, CH), eye8)
    kb = kb.reshape(L, S5_OCT, LANES, LANES)
    s_idx = jnp.arange(L)[:, None]
    t_idx = jnp.arange(L)[None, :]
    lag = t_idx - s_idx
    toe = jnp.where((lag >= 0)[:, :, None, None, None], kb[jnp.clip(lag, 0, L - 1)], 0.0)
    w_toe = toe.transpose(2, 0, 3, 1, 4).reshape(S5_OCT, L * LANES, L * LANES)

    prs, pis = pr[L - 1 - jnp.arange(L)], pi[L - 1 - jnp.arange(L)]
    vre = prs[..., None] * bbr[None] - pis[..., None] * bbi[None]
    vim = prs[..., None] * bbi[None] + pis[..., None] * bbr[None]

    def v_lay(z):
        z = z.reshape(L, S5_OCT, 8, P, CH)
        z = jnp.einsum('soipd,ik->soidkp', z, eye8)
        return z.reshape(L, S5_OCT, LANES, S5_OS).transpose(1, 0, 2, 3).reshape(S5_OCT, L * LANES, S5_OS)

    w_in = jnp.concatenate([v_lay(vre), v_lay(vim)], axis=-1)

    prt, pit = pr[1:L + 1], pi[1:L + 1]
    mre = cre[None] * prt[:, :, None, :] - cim[None] * pit[:, :, None, :]
    mim = -(cre[None] * pit[:, :, None, :] + cim[None] * prt[:, :, None, :])

    def m_lay(z):
        z = z.reshape(L, S5_OCT, 8, CH, P)
        z = jnp.einsum('toicp,ik->toipkc', z, eye8)
        return z.reshape(L, S5_OCT, S5_OS, LANES).transpose(1, 2, 0, 3).reshape(S5_OCT, S5_OS, L * LANES)

    w_out = jnp.concatenate([m_lay(mre), m_lay(mim)], axis=1)
    a_l = jnp.concatenate([pr[L].reshape(S5_OCT, 1, S5_OS), pi[L].reshape(S5_OCT, 1, S5_OS)], axis=-1)
    return w_toe.astype(BF16), w_in.astype(BF16), w_out.astype(BF16), a_l


def _s5_kernel(u_ref, wt_ref, wi_ref, wo_ref, al_ref, y_ref, s_ref, v_ref, sp_ref):
    @pl.when(pl.program_id(2) == 0)
    def _():
        s_ref[...] = jnp.zeros_like(s_ref)

    L, R = S5_L, S5_R
    lhs = jnp.concatenate([u_ref[pl.ds(s, R, stride=L), :] for s in range(L)], axis=1).astype(BF16)
    y_intra = jnp.dot(lhs, wt_ref[0], preferred_element_type=F32)
    v_ref[...] = jnp.dot(lhs, wi_ref[0], preferred_element_type=F32)
    ar = al_ref[0, :, :S5_OS]
    ai = al_ref[0, :, S5_OS:]

    def step(n, carry):
        sr, si = carry
        sp_ref[pl.ds(n, 1), :S5_OS] = sr
        sp_ref[pl.ds(n, 1), S5_OS:] = si
        vr = v_ref[pl.ds(n, 1), :S5_OS]
        vi = v_ref[pl.ds(n, 1), S5_OS:]
        return (ar * sr - ai * si + vr, ar * si + ai * sr + vi)

    sr, si = lax.fori_loop(0, R, step, (s_ref[:, :S5_OS], s_ref[:, S5_OS:]), unroll=8)
    s_ref[:, :S5_OS] = sr
    s_ref[:, S5_OS:] = si
    y = y_intra + jnp.dot(sp_ref[...].astype(BF16), wo_ref[0], preferred_element_type=F32)
    for t in range(L):
        y_ref[pl.ds(t, R, stride=L), :] = y[:, t * LANES:(t + 1) * LANES]


def _s5(zu, tables, l, batch, seq):
    w_toe, w_in, w_out, a_l = tables
    m = zu.shape[0]
    rows = S5_R * S5_L
    nr = seq // rows
    full = lambda shp: pl.BlockSpec((None, 1) + shp, lambda o, b, r: (l, o, 0, 0))
    return pl.pallas_call(
        _s5_kernel,
        grid=(S5_OCT, batch, nr),
        in_specs=[pl.BlockSpec((rows, LANES), lambda o, b, r: (b * nr + r, o)),
                  full((S5_L * LANES, S5_L * LANES)), full((S5_L * LANES, 2 * S5_OS)),
                  full((2 * S5_OS, S5_L * LANES)), full((1, 2 * S5_OS))],
        out_specs=pl.BlockSpec((rows, LANES), lambda o, b, r: (b * nr + r, o)),
        out_shape=jax.ShapeDtypeStruct((m, S5_W), F32),
        scratch_shapes=[pltpu.VMEM((1, 2 * S5_OS), F32),
                        pltpu.VMEM((S5_R, 2 * S5_OS), F32),
                        pltpu.VMEM((S5_R, 2 * S5_OS), F32)],
        compiler_params=_cparams(("parallel", "parallel", "arbitrary")),
        name="s5",
    )(zu, w_toe, w_in, w_out, a_l)


def _ret_tables(seq):
    C = RET_CHUNK
    pos = jnp.arange(seq, dtype=F32)
    inv_freq = 1.0 / (ROPE_BASE ** jnp.linspace(0.0, 1.0, RET_DK // 2, dtype=F32))
    ang = pos[:, None] * inv_freq[None, :]
    cos, sin = jnp.cos(ang), jnp.sin(ang)
    cos_t = jnp.concatenate([cos, cos, cos, cos], axis=1)
    sin_t = jnp.concatenate([-sin, sin, -sin, sin], axis=1)
    log_g = jnp.log(1.0 - 2.0 ** (-5.0 - jnp.arange(RET_HEADS, dtype=F32)))
    idx = jnp.arange(C, dtype=F32)
    rel = idx[:, None] - idx[None, :]
    dmat = jnp.where(rel[None] >= 0, jnp.exp(jnp.maximum(rel, 0.0)[None] * log_g[:, None, None]), 0.0)
    lane = jnp.arange(LANES)
    head_mask = ((lane[None, :] // RET_DK) == (jnp.arange(RET_HEADS) % 2)[:, None]).astype(F32)
    gq = jnp.exp((idx + 1.0)[None, :] * log_g[:, None])
    gk = jnp.exp((C - 1.0 - idx)[None, :] * log_g[:, None])
    qm = gq[:, :, None] * head_mask[:, None, :]
    km = gk[:, :, None] * head_mask[:, None, :]
    gc = jnp.exp(C * log_g)
    gc_t = jnp.broadcast_to(gc[:, None, None], (RET_HEADS, 1, LANES))
    return cos_t, sin_t, dmat, head_mask[:, None, :], qm, km, gc_t


def _ret_rows(q_ref, k_ref, v_ref, g_ref, cos_t, sin_t, dm_ref, hm_ref, qm_ref, km_ref, gc_ref, o_ref, st_ref):
    lane = lax.broadcasted_iota(jnp.int32, (RET_CHUNK, LANES), 1)
    first_half = (lane % RET_DK) < (RET_DK // 2)
    nt = (((1,), (1,)), ((), ()))
    tn = (((0,), (0,)), ((), ()))

    def rope(z):
        swapped = jnp.where(first_half, pltpu.roll(z, LANES - RET_DK // 2, 1), pltpu.roll(z, RET_DK // 2, 1))
        return z * cos_t + swapped * sin_t

    for tile in range(RET_QK // LANES):
        cols = slice(tile * LANES, (tile + 1) * LANES)
        q_t = rope(q_ref[:, cols])
        k_t = rope(k_ref[:, cols]) * (RET_DK ** -0.5)
        k_bf = k_t.astype(BF16)
        for half in range(2):
            h = 2 * tile + half
            vcols = slice(h * RET_DV, (h + 1) * RET_DV)
            v_bf = v_ref[:, vcols].astype(BF16)
            q_h = (q_t * hm_ref[h]).astype(BF16)
            scores = lax.dot_general(q_h, k_bf, nt, preferred_element_type=F32) * dm_ref[h]
            o = jnp.dot(scores.astype(BF16), v_bf, preferred_element_type=F32)
            st = st_ref[h]
            o = o + jnp.dot((q_t * qm_ref[h]).astype(BF16), st.astype(BF16), preferred_element_type=F32)
            k_dec = (k_t * km_ref[h]).astype(BF16)
            st_ref[h] = st * gc_ref[h] + lax.dot_general(k_dec, v_bf, tn, preferred_element_type=F32)
            gr = g_ref[:, vcols]
            o_ref[:, vcols] = _rms_f32(o) * (gr * jax.nn.sigmoid(gr))


def _mix_kernel(x_ref, g_ref, w_ref, lb_ref, on_ref, cos_ref, sin_ref, dm_ref, hm_ref, qm_ref, km_ref, gc_ref,
                ohg_ref, zu_ref, ort_ref, zh_ref, zr_ref, hst_ref, rst_ref, *, tiles_per_seq):
    @pl.when(pl.program_id(0) % tiles_per_seq == 0)
    def _():
        hst_ref[...] = jnp.zeros_like(hst_ref)
        rst_ref[...] = jnp.zeros_like(rst_ref)

    h = (_rms_f32(x_ref[...]) * g_ref[...]).astype(BF16)

    def proj(c0, n):
        return jnp.dot(h, w_ref[:, c0:c0 + n], preferred_element_type=F32)

    zu_ref[...] = proj(ZH_W, S5_W)
    onorm = on_ref[...]
    for pair in range(HG_HEADS // 2):
        views = []
        for kk in range(4):
            c0 = kk * HG_W + pair * HG_PAIR_W
            zh_ref[:, c0:c0 + HG_PAIR_W] = proj(c0, HG_PAIR_W)
            views.append(zh_ref.at[:, c0:c0 + HG_PAIR_W])
        pc = slice(pair * HG_PAIR_W, (pair + 1) * HG_PAIR_W)
        _hgrn_rows(*views, lb_ref[:, pc], onorm, ohg_ref.at[:, pc], hst_ref.at[pair], TM_IN)
    zr_ref[...] = proj(ZH_W + S5_W, ZR_W)
    _ret_rows(zr_ref.at[:, :RET_QK], zr_ref.at[:, RET_QK:2 * RET_QK],
              zr_ref.at[:, 2 * RET_QK:2 * RET_QK + RET_W], zr_ref.at[:, 2 * RET_QK + RET_W:],
              cos_ref[...], sin_ref[...], dm_ref, hm_ref, qm_ref, km_ref, gc_ref, ort_ref, rst_ref)


def _mix(x2, gain, w_bf, l, lb, onorm, ret_tables, seq):
    cos_t, sin_t, dmat, hmask, qm, km, gc_t = ret_tables
    m = x2.shape[0]
    assert TM_IN == RET_CHUNK and TM_IN % CHUNK == 0 and seq % TM_IN == 0
    tiles_per_seq = seq // TM_IN
    onorm2 = jnp.concatenate([onorm, onorm], axis=1)
    row = lambda w: pl.BlockSpec((TM_IN, w), lambda i: (i, 0))
    const2 = lambda a: pl.BlockSpec(a.shape, lambda i: (0, 0))
    const3 = lambda a: pl.BlockSpec(a.shape, lambda i: (0, 0, 0))
    pos = pl.BlockSpec((TM_IN, LANES), lambda i: (i % tiles_per_seq, 0))
    return pl.pallas_call(
        functools.partial(_mix_kernel, tiles_per_seq=tiles_per_seq),
        grid=(m // TM_IN,),
        in_specs=[row(D_MODEL), const2(gain),
                  pl.BlockSpec((None, D_MODEL, N_IN), lambda i: (l, 0, 0), pipeline_mode=pl.Buffered(1)),
                  const2(lb), const2(onorm2), pos, pos,
                  const3(dmat), const3(hmask), const3(qm), const3(km), const3(gc_t)],
        out_specs=[row(HG_W), row(S5_W), row(RET_W)],
        out_shape=[jax.ShapeDtypeStruct((m, HG_W), F32),
                   jax.ShapeDtypeStruct((m, S5_W), F32),
                   jax.ShapeDtypeStruct((m, RET_W), F32)],
        scratch_shapes=[pltpu.VMEM((TM_IN, ZH_W), F32),
                        pltpu.VMEM((TM_IN, ZR_W), F32),
                        pltpu.VMEM((HG_HEADS // 2, HG_PAIR_W, HG_PAIR_W), F32),
                        pltpu.VMEM((RET_HEADS, LANES, RET_DV), F32)],
        compiler_params=_cparams(("arbitrary",)),
        name="mix",
    )(x2, gain, w_bf, lb, onorm2, cos_t, sin_t, dmat, hmask, qm, km, gc_t)


def _outproj_kernel(oh_ref, ys_ref, zu_ref, or_ref, d_ref, wg_ref, bg_ref, beta_ref, w_ref, x_ref, g_ref,
                    out_ref):
    beta = beta_ref[...]
    y = ys_ref[...] + d_ref[...] * zu_ref[...]
    y = 0.5 * y * (1.0 + lax.erf(y * (2.0 ** -0.5)))
    glu = jnp.dot(y.astype(BF16), wg_ref[...], preferred_element_type=F32) + bg_ref[...]
    o_s5 = y * jax.nn.sigmoid(glu)
    acc = jnp.dot((oh_ref[...] * beta[:, :HG_W]).astype(BF16), w_ref[:HG_W, :], preferred_element_type=F32)
    acc += jnp.dot((o_s5 * beta[:, HG_W:HG_W + S5_W]).astype(BF16), w_ref[HG_W:HG_W + S5_W, :],
                   preferred_element_type=F32)
    acc += jnp.dot((or_ref[...] * beta[:, HG_W + S5_W:]).astype(BF16), w_ref[HG_W + S5_W:, :],
                   preferred_element_type=F32)
    out_ref[...] = x_ref[...] + _rms_f32(acc) * g_ref[...]


def _outproj(o_hg, y_s5, zu, o_rt, d_skip, w_glu_bf, b_glu, beta, w_bf, x2, gain, l):
    m = x2.shape[0]
    row = lambda w: pl.BlockSpec((TM_OUT, w), lambda i: (i, 0))
    const = lambda r, c: pl.BlockSpec((r, c), lambda i: (0, 0))
    layer = lambda r, c: pl.BlockSpec((None, r, c), lambda i: (l, 0, 0))
    return pl.pallas_call(
        _outproj_kernel,
        grid=(m // TM_OUT,),
        in_specs=[row(HG_W), row(S5_W), row(S5_W), row(RET_W),
                  const(1, S5_W), layer(S5_W, S5_W), const(1, S5_W), const(1, D_MIX),
                  layer(D_MIX, D_MODEL), row(D_MODEL), const(1, D_MODEL)],
        out_specs=row(D_MODEL),
        out_shape=jax.ShapeDtypeStruct((m, D_MODEL), F32),
        compiler_params=_cparams(("parallel",)),
        name="outproj",
    )(o_hg, y_s5, zu, o_rt, d_skip, w_glu_bf, b_glu, beta, w_bf, x2, gain)


def _ffn_kernel(x_ref, xh_ref, gpre_ref, wg_ref, wv_ref, cwg_ref, cwv_ref, cbg_ref, cbv_ref, wd_ref,
                gpost_ref, out_ref, h_ref, ug_ref, uv_ref, *, tiles_per_seq):
    i = pl.program_id(0)
    j = pl.program_id(1)

    @pl.when(j == 0)
    def _():
        gpre = gpre_ref[...]
        h_ref[FFN_HALO:, :] = (_rms_f32(x_ref[...]) * gpre).astype(BF16)
        halo = (_rms_f32(xh_ref[...]) * gpre).astype(BF16)
        first = (i % tiles_per_seq) == 0
        h_ref[:FFN_HALO, :] = jnp.where(first, jnp.zeros_like(halo), halo)
        out_ref[...] = jnp.zeros_like(out_ref)

    edges = [0] + list(range(FFN_HALO + FFN_STRIP, TM_FFN + FFN_HALO + 1, FFN_STRIP))
    for lo, hi in zip(edges[:-1], edges[1:]):
        h = h_ref[lo:hi, :]
        ug_ref[lo:hi, :] = jnp.dot(h, wg_ref[...], preferred_element_type=F32)
        uv_ref[lo:hi, :] = jnp.dot(h, wv_ref[...], preferred_element_type=F32)
    cwg, cwv, cbg, cbv = cwg_ref[...], cwv_ref[...], cbg_ref[...], cbv_ref[...]

    def conv(u_ref, cw, cb, r0):
        taps = [u_ref[pl.ds(FFN_HALO - 2 + tap + r0, FFN_STRIP), :] for tap in range(3)]
        return cw[0:1] * taps[0] + cw[1:2] * taps[1] + cw[2:3] * taps[2] + cb

    for s in range(TM_FFN // FFN_STRIP):
        r0 = s * FFN_STRIP
        gate = conv(ug_ref, cwg, cbg, r0)
        val = conv(uv_ref, cwv, cbv, r0)
        act = (gate * jax.nn.sigmoid(gate) * val).astype(BF16)
        out_ref[pl.ds(r0, FFN_STRIP), :] += jnp.dot(act, wd_ref[...], preferred_element_type=F32)

    @pl.when(j == pl.num_programs(1) - 1)
    def _():
        out_ref[...] = x_ref[...] + _rms_f32(out_ref[...]) * gpost_ref[...]


def _ffn(x2, gpre, w_up_bf, conv_w, conv_b, w_down_bf, gpost, seq, l):
    m = x2.shape[0]
    tm, tn = TM_FFN, TN_FFN
    nj = D_FF // tn
    hb = tm // FFN_HALO
    const = lambda r, c: pl.BlockSpec((r, c), lambda i, j: (0, 0))
    return pl.pallas_call(
        functools.partial(_ffn_kernel, tiles_per_seq=seq // tm),
        grid=(m // tm, nj),
        in_specs=[pl.BlockSpec((tm, D_MODEL), lambda i, j: (i, 0)),
                  pl.BlockSpec((FFN_HALO, D_MODEL), lambda i, j: (jnp.maximum(i * hb - 1, 0), 0)),
                  const(1, D_MODEL),
                  pl.BlockSpec((None, D_MODEL, tn), lambda i, j: (l, 0, j)),
                  pl.BlockSpec((None, D_MODEL, tn), lambda i, j: (l, 0, nj + j)),
                  pl.BlockSpec((3, tn), lambda i, j: (0, j)),
                  pl.BlockSpec((3, tn), lambda i, j: (0, nj + j)),
                  pl.BlockSpec((1, tn), lambda i, j: (0, j)),
                  pl.BlockSpec((1, tn), lambda i, j: (0, nj + j)),
                  pl.BlockSpec((None, tn, D_MODEL), lambda i, j: (l, j, 0)),
                  const(1, D_MODEL)],
        out_specs=pl.BlockSpec((tm, D_MODEL), lambda i, j: (i, 0)),
        out_shape=jax.ShapeDtypeStruct((m, D_MODEL), F32),
        scratch_shapes=[pltpu.VMEM((tm + FFN_HALO, D_MODEL), BF16),
                        pltpu.VMEM((tm + FFN_HALO, tn), F32),
                        pltpu.VMEM((tm + FFN_HALO, tn), F32)],
        compiler_params=_cparams(("parallel", "arbitrary")),
        name="ffn",
    )(x2, x2, gpre, w_up_bf, w_up_bf, conv_w, conv_w, conv_b, conv_b, w_down_bf, gpost)


def kernel(x, w_in, w_out, mix_beta, hg_lb_logits, hg_onorm, s5_lam_re, s5_lam_im, s5_log_dt, s5_b_re,
           s5_b_im, s5_c_re, s5_c_im, s5_d, s5_w_glu, s5_b_glu, ffn_w_up, ffn_conv_w, ffn_conv_b,
           ffn_w_down, g_pre_mix, g_post_mix, g_pre_ffn, g_post_ffn):
    batch, seq, d = x.shape
    depth = w_in.shape[0]
    assert d == D_MODEL and seq % (S5_R * S5_L) == 0 and seq % TM_FFN == 0
    p = jax.nn.softmax(hg_lb_logits.astype(F32), axis=0)
    cum = jnp.cumsum(p, axis=0)
    lower_bounds = cum - cum[0:1]
    ret_tables = _ret_tables(seq)
    row = lambda a: a.astype(F32).reshape(1, -1)

    w_in_bf, w_out_bf, w_glu_bf = w_in.astype(BF16), w_out.astype(BF16), s5_w_glu.astype(BF16)
    w_up_bf, w_down_bf = ffn_w_up.astype(BF16), ffn_w_down.astype(BF16)
    s5_tabs = jax.vmap(_s5_tables)(s5_lam_re, s5_lam_im, s5_log_dt, s5_b_re, s5_b_im, s5_c_re, s5_c_im)
    x2 = x.reshape(batch * seq, d)
    for l in range(depth):
        o_hg, zu, o_rt = _mix(x2, row(g_pre_mix[l]), w_in_bf, l, row(lower_bounds[l]), row(hg_onorm[l]),
                              ret_tables, seq)
        y_s5 = _s5(zu, s5_tabs, l, batch, seq)
        x2 = _outproj(o_hg, y_s5, zu, o_rt, row(s5_d[l]), w_glu_bf, row(s5_b_glu[l]),
                      row(mix_beta[l]), w_out_bf, x2, row(g_post_mix[l]), l)
        x2 = _ffn(x2, row(g_pre_ffn[l]), w_up_bf, ffn_conv_w[l].astype(F32),
                  row(ffn_conv_b[l]), w_down_bf, row(g_post_ffn[l]), seq, l)
    return x2.reshape(batch, seq, d)
```

```python
import functools
import math

import jax
import jax.numpy as jnp
from jax import lax
from jax.experimental import pallas as pl
from jax.experimental.pallas import tpu as pltpu

F32 = jnp.float32
BF16 = jnp.bfloat16

D_MODEL = 2048
CHUNK = 64
HG_HEADS = 6
HG_D = 128
HG_W = HG_HEADS * HG_D
S5_GROUP_CH = 16
S5_W = 512
S5_GROUPS = 32
S5_STATE = 64
RET_HEADS = 6
RET_DK = 64
RET_DV = 128
RET_W = RET_HEADS * RET_DV
RET_QK = RET_HEADS * RET_DK
D_MIX = HG_W + S5_W + RET_W
ZH_W = 4 * HG_W
ZR_W = 2 * RET_QK + 2 * RET_W
N_IN = ZH_W + S5_W + ZR_W
D_FF = 5632
NORM_EPS = 1e-6
ROPE_BASE = 10000.0

LANES = 128
SUBL = 8
LOG2E = math.log2(math.e)
VMEM_LIMIT = 60000 * 1024

TM_IN = 256
TM_OUT = 512
TM_FFN = 1024
TN_FFN = 512
FFN_HALO = 16
FFN_STRIP = 256
HG_LEVELS = (32, 16, 8, 4, 2, 1)
HG_PAIR_W = 2 * HG_D
RET_CHUNK = 256
S5_L = 8
S5_R = 256
S5_OCT = 4
S5_OS = 8 * S5_STATE


def _cparams(sem):
    return pltpu.CompilerParams(dimension_semantics=sem, vmem_limit_bytes=VMEM_LIMIT)


def _rms_f32(x):
    return x * lax.rsqrt(jnp.mean(x * x, axis=-1, keepdims=True) + NORM_EPS)


def _hgrn_level_exponents(log_f):
    nv = CHUNK // SUBL
    w = log_f.shape[-1]
    x = log_f.reshape(nv, SUBL, w)
    sub = lax.broadcasted_iota(jnp.int32, (nv, SUBL, w), 1)
    loc = x
    for d in (1, 2, 4):
        loc = loc + jnp.where(sub >= d, pltpu.roll(loc, d, 1), 0.0)
    bc = lambda a, r: jnp.broadcast_to(a[:, r:r + 1, :], a.shape)
    tot = bc(loc, SUBL - 1)
    suf = tot - loc
    tots = [tot[j] for j in range(nv)]
    locs = [loc[j] for j in range(nv)]
    sufs = [suf[j] for j in range(nv)]

    def span(js):
        acc = None
        for j in js:
            acc = tots[j] if acc is None else acc + tots[j]
        return acc

    def plus(a, s):
        return a if s is None else a + s

    b = jnp.stack([plus(locs[j], span(range(j))) for j in range(nv)])
    b_rev = jnp.stack([plus(sufs[j], span(range(j + 1, nv))) for j in range(nv)])
    levels = []
    for h in HG_LEVELS:
        if h >= SUBL:
            g = h // SUBL
            rows = []
            for j in range(nv):
                blk = j // g
                if blk % 2 == 1:
                    rows.append(plus(locs[j], span(range(blk * g, j))))
                else:
                    rows.append(plus(sufs[j], span(range(j + 1, (blk + 1) * g))))
            levels.append(jnp.stack(rows))
        elif h == 4:
            ref = bc(loc, 3)
            levels.append(jnp.where(sub >= 4, loc - ref, ref - loc))
        elif h == 2:
            ref = jnp.where(sub < 4, bc(loc, 1), bc(loc, 5))
            levels.append(jnp.where((sub & 3) >= 2, loc - ref, ref - loc))
        else:
            levels.append(jnp.where((sub & 1) == 1, x, 0.0))
    flat = lambda a: a.reshape(CHUNK, w)
    return flat(b), flat(b_rev), [flat(l) for l in levels]


def _hgrn_consts():
    t2 = lax.broadcasted_iota(jnp.int32, (CHUNK, HG_D), 0)
    s2 = lax.broadcasted_iota(jnp.int32, (CHUNK, HG_D), 1) & (CHUNK - 1)
    top_bit = t2 ^ s2
    level_masks = [(top_bit >= h) & (top_bit < 2 * h) & (t2 > s2) for h in HG_LEVELS]
    diag_mask = t2 == s2
    r = lax.broadcasted_iota(jnp.int32, (HG_PAIR_W, HG_PAIR_W), 0) >= HG_D
    c = lax.broadcasted_iota(jnp.int32, (HG_PAIR_W, HG_PAIR_W), 1) >= HG_D
    return level_masks, diag_mask, r == c, jnp.zeros((CHUNK, HG_D), BF16)


def _hgrn_chunk(q_ref, f_ref, i_ref, g_ref, lb, onorm, o_ref, st_ref, ch, consts):
    level_masks, diag_mask, same_head, zeros_bf = consts
    log_lb = jnp.log(lb)
    log_1mlb = jnp.log1p(-lb)
    one_mlb = 1.0 - lb
    nt = (((1,), (1,)), ((), ()))
    tn = (((0,), (0,)), ((), ()))

    def block_diag_rows(a):
        return jnp.concatenate([jnp.concatenate([a[:, :HG_D], zeros_bf], axis=1),
                                jnp.concatenate([zeros_bf, a[:, HG_D:]], axis=1)], axis=0)

    rows = pl.ds(ch * CHUNK, CHUNK)
    qr = q_ref[rows, :]
    fr = f_ref[rows, :]
    gr = g_ref[rows, :]
    v_bf = i_ref[rows, :].astype(BF16)
    q = qr * jax.nn.sigmoid(qr)
    e = jnp.exp(-jnp.abs(fr))
    log_sig = jnp.minimum(fr, 0.0) - jnp.log(1.0 + e)
    cc = log_1mlb + log_sig
    log_f = jnp.maximum(log_lb, cc) + jnp.log(1.0 + jnp.exp(-jnp.abs(log_lb - cc)))
    k = one_mlb * (jnp.where(fr >= 0.0, e, 1.0) / (1.0 + e))
    b, b_rev, level_exps = _hgrn_level_exponents(log_f * LOG2E)
    e_b = jnp.exp2(b)
    q_bf = q.astype(BF16)
    k_bf = k.astype(BF16)
    st = st_ref[...]
    o = lax.dot_general((q * e_b).astype(BF16), st.astype(BF16), nt, preferred_element_type=F32)
    k_dec = (k * jnp.exp2(b_rev)).astype(BF16)
    kv = lax.dot_general(v_bf, k_dec, tn, preferred_element_type=F32)
    st_ref[...] = st * e_b[CHUNK - 1:CHUNK, :] + jnp.where(same_head, kv, 0.0)
    a = jnp.where(diag_mask, lax.dot_general(q_bf, block_diag_rows(k_bf), nt,
                                             preferred_element_type=F32), 0.0)
    for d_h, mask in zip(level_exps, level_masks):
        e_h = jnp.exp2(d_h).astype(BF16)
        p = lax.dot_general(q_bf * e_h, block_diag_rows(k_bf * e_h), nt, preferred_element_type=F32)
        a = jnp.where(mask, p, a)
    o = o + jnp.dot(a.astype(BF16), block_diag_rows(v_bf), preferred_element_type=F32)
    gate = gr * jax.nn.sigmoid(gr)
    for hh in range(2):
        cols = slice(hh * HG_D, (hh + 1) * HG_D)
        o_ref[rows, cols] = _rms_f32(o[:, cols]) * onorm[:, cols] * gate[:, cols]


def _s5_tables(lam_re, lam_im, log_dt, b_re, b_im, c_re, c_im):
    hp = lax.Precision.HIGHEST
    L = S5_L
    dt = jnp.exp(log_dt.astype(F32))[:, None]
    lr = jnp.minimum(lam_re.astype(F32), -1e-4)
    li = lam_im.astype(F32)
    mag = jnp.exp(lr * dt)
    ar, ai = mag * jnp.cos(li * dt), mag * jnp.sin(li * dt)
    den = lr * lr + li * li
    nr_, ni_ = ar - 1.0, ai
    cr, ci = (nr_ * lr + ni_ * li) / den, (ni_ * lr - nr_ * li) / den
    br_, bi_ = b_re.astype(F32), b_im.astype(F32)
    bbr = cr[..., None] * br_ - ci[..., None] * bi_
    bbi = cr[..., None] * bi_ + ci[..., None] * br_
    cre, cim = c_re.astype(F32), c_im.astype(F32)
    jj = jnp.arange(L + 1, dtype=F32)[:, None, None]
    pmag = jnp.exp(lr * dt * jj)
    pr, pi = pmag * jnp.cos(li * dt * jj), pmag * jnp.sin(li * dt * jj)

    G, P, CH = S5_GROUPS, S5_STATE, S5_GROUP_CH
    ein = functools.partial(jnp.einsum, precision=hp)
    kj = (ein('gcp,jgp,gpd->jgcd', cre, pr[:L], bbr) - ein('gcp,jgp,gpd->jgcd', cre, pi[:L], bbi)
          - ein('gcp,jgp,gpd->jgcd', cim, pr[:L], bbi) - ein('gcp,jgp,gpd->jgcd', cim, pi[:L], bbr))
    eye8 = jnp.eye(8, dtype=F32)
    kb = jnp.einsum('joicd,ik->joidkc', kj.reshape(L, S5_OCT, 8, CH, CH), eye8)
    kb = kb.reshape(L, S5_OCT, LANES, LANES)
    s_idx = jnp.arange(L)[:, None]
    t_idx = jnp.arange(L)[None, :]
    lag = t_idx - s_idx
    toe = jnp.where((lag >= 0)[:, :, None, None, None], kb[jnp.clip(lag, 0, L - 1)], 0.0)
    w_toe = toe.transpose(2, 0, 3, 1, 4).reshape(S5_OCT, L * LANES, L * LANES)

    prs, pis = pr[L - 1 - jnp.arange(L)], pi[L - 1 - jnp.arange(L)]
    vre = prs[..., None] * bbr[None] - pis[..., None] * bbi[None]
    vim = prs[..., None] * bbi[None] + pis[..., None] * bbr[None]

    def v_lay(z):
        z = z.reshape(L, S5_OCT, 8, P, CH)
        z = jnp.einsum('soipd,ik->soidkp', z, eye8)
        return z.reshape(L, S5_OCT, LANES, S5_OS).transpose(1, 0, 2, 3).reshape(S5_OCT, L * LANES, S5_OS)

    w_in = jnp.concatenate([v_lay(vre), v_lay(vim)], axis=-1)

    prt, pit = pr[1:L + 1], pi[1:L + 1]
    mre = cre[None] * prt[:, :, None, :] - cim[None] * pit[:, :, None, :]
    mim = -(cre[None] * pit[:, :, None, :] + cim[None] * prt[:, :, None, :])

    def m_lay(z):
        z = z.reshape(L, S5_OCT, 8, CH, P)
        z = jnp.einsum('toicp,ik->toipkc', z, eye8)
        return z.reshape(L, S5_OCT, S5_OS, LANES).transpose(1, 2, 0, 3).reshape(S5_OCT, S5_OS, L * LANES)

    w_out = jnp.concatenate([m_lay(mre), m_lay(mim)], axis=1)
    a_l = jnp.concatenate([pr[L].reshape(S5_OCT, 1, S5_OS), pi[L].reshape(S5_OCT, 1, S5_OS)], axis=-1)
    return w_toe.astype(BF16), w_in.astype(BF16), w_out.astype(BF16), a_l


def _s5_kernel(u_ref, wt_ref, wi_ref, wo_ref, al_ref, y_ref, s_ref, v_ref, sp_ref):
    @pl.when(pl.program_id(2) == 0)
    def _():
        s_ref[...] = jnp.zeros_like(s_ref)

    L, R = S5_L, S5_R
    lhs = jnp.concatenate([u_ref[pl.ds(s, R, stride=L), :] for s in range(L)], axis=1).astype(BF16)
    y_intra = jnp.dot(lhs, wt_ref[0], preferred_element_type=F32)
    v_ref[...] = jnp.dot(lhs, wi_ref[0], preferred_element_type=F32)
    ar = al_ref[0, :, :S5_OS]
    ai = al_ref[0, :, S5_OS:]

    def step(n, carry):
        sr, si = carry
        sp_ref[pl.ds(n, 1), :S5_OS] = sr
        sp_ref[pl.ds(n, 1), S5_OS:] = si
        vr = v_ref[pl.ds(n, 1), :S5_OS]
        vi = v_ref[pl.ds(n, 1), S5_OS:]
        return (ar * sr - ai * si + vr, ar * si + ai * sr + vi)

    sr, si = lax.fori_loop(0, R, step, (s_ref[:, :S5_OS], s_ref[:, S5_OS:]), unroll=8)
    s_ref[:, :S5_OS] = sr
    s_ref[:, S5_OS:] = si
    y = y_intra + jnp.dot(sp_ref[...].astype(BF16), wo_ref[0], preferred_element_type=F32)
    for t in range(L):
        y_ref[pl.ds(t, R, stride=L), :] = y[:, t * LANES:(t + 1) * LANES]


def _s5(zu, tables, l, batch, seq):
    w_toe, w_in, w_out, a_l = tables
    m = zu.shape[0]
    rows = S5_R * S5_L
    nr = seq // rows
    full = lambda shp: pl.BlockSpec((None, 1) + shp, lambda o, b, r: (l, o, 0, 0))
    return pl.pallas_call(
        _s5_kernel,
        grid=(S5_OCT, batch, nr),
        in_specs=[pl.BlockSpec((rows, LANES), lambda o, b, r: (b * nr + r, o)),
                  full((S5_L * LANES, S5_L * LANES)), full((S5_L * LANES, 2 * S5_OS)),
                  full((2 * S5_OS, S5_L * LANES)), full((1, 2 * S5_OS))],
        out_specs=pl.BlockSpec((rows, LANES), lambda o, b, r: (b * nr + r, o)),
        out_shape=jax.ShapeDtypeStruct((m, S5_W), F32),
        scratch_shapes=[pltpu.VMEM((1, 2 * S5_OS), F32),
                        pltpu.VMEM((S5_R, 2 * S5_OS), F32),
                        pltpu.VMEM((S5_R, 2 * S5_OS), F32)],
        compiler_params=_cparams(("parallel", "parallel", "arbitrary")),
        name="s5",
    )(zu, w_toe, w_in, w_out, a_l)


def _ret_tables(seq):
    C = RET_CHUNK
    pos = jnp.arange(seq, dtype=F32)
    inv_freq = 1.0 / (ROPE_BASE ** jnp.linspace(0.0, 1.0, RET_DK // 2, dtype=F32))
    ang = pos[:, None] * inv_freq[None, :]
    cos, sin = jnp.cos(ang), jnp.sin(ang)
    cos_t = jnp.concatenate([cos, cos, cos, cos], axis=1)
    sin_t = jnp.concatenate([-sin, sin, -sin, sin], axis=1)
    log_g = jnp.log(1.0 - 2.0 ** (-5.0 - jnp.arange(RET_HEADS, dtype=F32)))
    idx = jnp.arange(C, dtype=F32)
    rel = idx[:, None] - idx[None, :]
    dmat = jnp.where(rel[None] >= 0, jnp.exp(jnp.maximum(rel, 0.0)[None] * log_g[:, None, None]), 0.0)
    lane = jnp.arange(LANES)
    head_mask = ((lane[None, :] // RET_DK) == (jnp.arange(RET_HEADS) % 2)[:, None]).astype(F32)
    gq = jnp.exp((idx + 1.0)[None, :] * log_g[:, None])
    gk = jnp.exp((C - 1.0 - idx)[None, :] * log_g[:, None])
    qm = gq[:, :, None] * head_mask[:, None, :]
    km = gk[:, :, None] * head_mask[:, None, :]
    gc = jnp.exp(C * log_g)
    gc_t = jnp.broadcast_to(gc[:, None, None], (RET_HEADS, 1, LANES))
    return cos_t, sin_t, dmat, head_mask[:, None, :], qm, km, gc_t


def _ret_rows(q_ref, k_ref, v_ref, g_ref, cos_t, sin_t, dm_ref, hm_ref, qm_ref, km_ref, gc_ref, o_ref, st_ref):
    lane = lax.broadcasted_iota(jnp.int32, (RET_CHUNK, LANES), 1)
    first_half = (lane % RET_DK) < (RET_DK // 2)
    nt = (((1,), (1,)), ((), ()))
    tn = (((0,), (0,)), ((), ()))

    def rope(z):
        swapped = jnp.where(first_half, pltpu.roll(z, LANES - RET_DK // 2, 1), pltpu.roll(z, RET_DK // 2, 1))
        return z * cos_t + swapped * sin_t

    for tile in range(RET_QK // LANES):
        cols = slice(tile * LANES, (tile + 1) * LANES)
        q_t = rope(q_ref[:, cols])
        k_t = rope(k_ref[:, cols]) * (RET_DK ** -0.5)
        k_bf = k_t.astype(BF16)
        for half in range(2):
            h = 2 * tile + half
            vcols = slice(h * RET_DV, (h + 1) * RET_DV)
            v_bf = v_ref[:, vcols].astype(BF16)
            q_h = (q_t * hm_ref[h]).astype(BF16)
            scores = lax.dot_general(q_h, k_bf, nt, preferred_element_type=F32) * dm_ref[h]
            o = jnp.dot(scores.astype(BF16), v_bf, preferred_element_type=F32)
            st = st_ref[h]
            o = o + jnp.dot((q_t * qm_ref[h]).astype(BF16), st.astype(BF16), preferred_element_type=F32)
            k_dec = (k_t * km_ref[h]).astype(BF16)
            st_ref[h] = st * gc_ref[h] + lax.dot_general(k_dec, v_bf, tn, preferred_element_type=F32)
            gr = g_ref[:, vcols]
            o_ref[:, vcols] = _rms_f32(o) * (gr * jax.nn.sigmoid(gr))


def _mix_kernel(x_ref, g_ref, w_ref, lb_ref, on_ref, cos_ref, sin_ref, dm_ref, hm_ref, qm_ref, km_ref, gc_ref,
                ohg_ref, zu_ref, ort_ref, zh_ref, zr_ref, hst_ref, rst_ref, *, tiles_per_seq):
    @pl.when(pl.program_id(0) % tiles_per_seq == 0)
    def _():
        hst_ref[...] = jnp.zeros_like(hst_ref)
        rst_ref[...] = jnp.zeros_like(rst_ref)

    h = (_rms_f32(x_ref[...]) * g_ref[...]).astype(BF16)

    def proj(c0, n):
        return jnp.dot(h, w_ref[:, c0:c0 + n], preferred_element_type=F32)

    zu_ref[...] = proj(ZH_W, S5_W)
    pair_views = []
    for pair in range(HG_HEADS // 2):
        views = []
        for kk in range(4):
            c0 = kk * HG_W + pair * HG_PAIR_W
            zh_ref[:, c0:c0 + HG_PAIR_W] = proj(c0, HG_PAIR_W)
            views.append(zh_ref.at[:, c0:c0 + HG_PAIR_W])
        pair_views.append(views)
    zr_ref[...] = proj(ZH_W + S5_W, ZR_W)
    onorm = on_ref[...]
    consts = _hgrn_consts()
    for ch in range(TM_IN // CHUNK):
        for pair in range(HG_HEADS // 2):
            pc = slice(pair * HG_PAIR_W, (pair + 1) * HG_PAIR_W)
            _hgrn_chunk(*pair_views[pair], lb_ref[:, pc], onorm, ohg_ref.at[:, pc], hst_ref.at[pair], ch,
                        consts)
    _ret_rows(zr_ref.at[:, :RET_QK], zr_ref.at[:, RET_QK:2 * RET_QK],
              zr_ref.at[:, 2 * RET_QK:2 * RET_QK + RET_W], zr_ref.at[:, 2 * RET_QK + RET_W:],
              cos_ref[...], sin_ref[...], dm_ref, hm_ref, qm_ref, km_ref, gc_ref, ort_ref, rst_ref)


def _mix(x2, gain, w_bf, l, lb, onorm, ret_tables, seq):
    cos_t, sin_t, dmat, hmask, qm, km, gc_t = ret_tables
    m = x2.shape[0]
    assert TM_IN == RET_CHUNK and TM_IN % CHUNK == 0 and seq % TM_IN == 0
    tiles_per_seq = seq // TM_IN
    onorm2 = jnp.concatenate([onorm, onorm], axis=1)
    row = lambda w: pl.BlockSpec((TM_IN, w), lambda i: (i, 0))
    const2 = lambda a: pl.BlockSpec(a.shape, lambda i: (0, 0))
    const3 = lambda a: pl.BlockSpec(a.shape, lambda i: (0, 0, 0))
    pos = pl.BlockSpec((TM_IN, LANES), lambda i: (i % tiles_per_seq, 0))
    return pl.pallas_call(
        functools.partial(_mix_kernel, tiles_per_seq=tiles_per_seq),
        grid=(m // TM_IN,),
        in_specs=[row(D_MODEL), const2(gain),
                  pl.BlockSpec((None, D_MODEL, N_IN), lambda i: (l, 0, 0), pipeline_mode=pl.Buffered(1)),
                  const2(lb), const2(onorm2), pos, pos,
                  const3(dmat), const3(hmask), const3(qm), const3(km), const3(gc_t)],
        out_specs=[row(HG_W), row(S5_W), row(RET_W)],
        out_shape=[jax.ShapeDtypeStruct((m, HG_W), F32),
                   jax.ShapeDtypeStruct((m, S5_W), F32),
                   jax.ShapeDtypeStruct((m, RET_W), F32)],
        scratch_shapes=[pltpu.VMEM((TM_IN, ZH_W), F32),
                        pltpu.VMEM((TM_IN, ZR_W), F32),
                        pltpu.VMEM((HG_HEADS // 2, HG_PAIR_W, HG_PAIR_W), F32),
                        pltpu.VMEM((RET_HEADS, LANES, RET_DV), F32)],
        compiler_params=_cparams(("arbitrary",)),
        name="mix",
    )(x2, gain, w_bf, lb, onorm2, cos_t, sin_t, dmat, hmask, qm, km, gc_t)


def _outproj_kernel(oh_ref, ys_ref, zu_ref, or_ref, d_ref, wg_ref, bg_ref, beta_ref, w_ref, x_ref, g_ref,
                    out_ref):
    beta = beta_ref[...]
    y = ys_ref[...] + d_ref[...] * zu_ref[...]
    y = 0.5 * y * (1.0 + lax.erf(y * (2.0 ** -0.5)))
    glu = jnp.dot(y.astype(BF16), wg_ref[...], preferred_element_type=F32) + bg_ref[...]
    o_s5 = y * jax.nn.sigmoid(glu)
    acc = jnp.dot((oh_ref[...] * beta[:, :HG_W]).astype(BF16), w_ref[:HG_W, :], preferred_element_type=F32)
    acc += jnp.dot((o_s5 * beta[:, HG_W:HG_W + S5_W]).astype(BF16), w_ref[HG_W:HG_W + S5_W, :],
                   preferred_element_type=F32)
    acc += jnp.dot((or_ref[...] * beta[:, HG_W + S5_W:]).astype(BF16), w_ref[HG_W + S5_W:, :],
                   preferred_element_type=F32)
    out_ref[...] = x_ref[...] + _rms_f32(acc) * g_ref[...]


def _outproj(o_hg, y_s5, zu, o_rt, d_skip, w_glu_bf, b_glu, beta, w_bf, x2, gain, l):
    m = x2.shape[0]
    row = lambda w: pl.BlockSpec((TM_OUT, w), lambda i: (i, 0))
    const = lambda r, c: pl.BlockSpec((r, c), lambda i: (0, 0))
    layer = lambda r, c: pl.BlockSpec((None, r, c), lambda i: (l, 0, 0))
    return pl.pallas_call(
        _outproj_kernel,
        grid=(m // TM_OUT,),
        in_specs=[row(HG_W), row(S5_W), row(S5_W), row(RET_W),
                  const(1, S5_W), layer(S5_W, S5_W), const(1, S5_W), const(1, D_MIX),
                  layer(D_MIX, D_MODEL), row(D_MODEL), const(1, D_MODEL)],
        out_specs=row(D_MODEL),
        out_shape=jax.ShapeDtypeStruct((m, D_MODEL), F32),
        compiler_params=_cparams(("parallel",)),
        name="outproj",
    )(o_hg, y_s5, zu, o_rt, d_skip, w_glu_bf, b_glu, beta, w_bf, x2, gain)


def _ffn_kernel(x_ref, xh_ref, gpre_ref, wg_ref, wv_ref, cwg_ref, cwv_ref, cbg_ref, cbv_ref, wd_ref,
                gpost_ref, out_ref, h_ref, ug_ref, uv_ref, *, tiles_per_seq):
    i = pl.program_id(0)
    j = pl.program_id(1)

    @pl.when(j == 0)
    def _():
        gpre = gpre_ref[...]
        for r0 in range(0, TM_FFN, FFN_STRIP):
            h_ref[pl.ds(FFN_HALO + r0, FFN_STRIP), :] = (
                _rms_f32(x_ref[pl.ds(r0, FFN_STRIP), :]) * gpre).astype(BF16)
        halo = (_rms_f32(xh_ref[...]) * gpre).astype(BF16)
        first = (i % tiles_per_seq) == 0
        h_ref[:FFN_HALO, :] = jnp.where(first, jnp.zeros_like(halo), halo)
        out_ref[...] = jnp.zeros_like(out_ref)

    edges = [0] + list(range(FFN_HALO + FFN_STRIP, TM_FFN + FFN_HALO + 1, FFN_STRIP))
    for lo, hi in zip(edges[:-1], edges[1:]):
        h = h_ref[lo:hi, :]
        ug_ref[lo:hi, :] = jnp.dot(h, wg_ref[...], preferred_element_type=F32)
        uv_ref[lo:hi, :] = jnp.dot(h, wv_ref[...], preferred_element_type=F32)
    cwg, cwv, cbg, cbv = cwg_ref[...], cwv_ref[...], cbg_ref[...], cbv_ref[...]

    def conv(u_ref, cw, cb, r0):
        taps = [u_ref[pl.ds(FFN_HALO - 2 + tap + r0, FFN_STRIP), :] for tap in range(3)]
        return cw[0:1] * taps[0] + cw[1:2] * taps[1] + cw[2:3] * taps[2] + cb

    for s in range(TM_FFN // FFN_STRIP):
        r0 = s * FFN_STRIP
        gate = conv(ug_ref, cwg, cbg, r0)
        val = conv(uv_ref, cwv, cbv, r0)
        act = (gate * jax.nn.sigmoid(gate) * val).astype(BF16)
        out_ref[pl.ds(r0, FFN_STRIP), :] += jnp.dot(act, wd_ref[...], preferred_element_type=F32)

    @pl.when(j == pl.num_programs(1) - 1)
    def _():
        gpost = gpost_ref[...]
        for r0 in range(0, TM_FFN, FFN_STRIP):
            rr = pl.ds(r0, FFN_STRIP)
            out_ref[rr, :] = x_ref[rr, :] + _rms_f32(out_ref[rr, :]) * gpost


def _ffn(x2, gpre, w_up_bf, conv_w, conv_b, w_down_bf, gpost, seq, l):
    m = x2.shape[0]
    tm, tn = TM_FFN, TN_FFN
    nj = D_FF // tn
    hb = tm // FFN_HALO
    const = lambda r, c: pl.BlockSpec((r, c), lambda i, j: (0, 0))
    return pl.pallas_call(
        functools.partial(_ffn_kernel, tiles_per_seq=seq // tm),
        grid=(m // tm, nj),
        in_specs=[pl.BlockSpec((tm, D_MODEL), lambda i, j: (i, 0)),
                  pl.BlockSpec((FFN_HALO, D_MODEL), lambda i, j: (jnp.maximum(i * hb - 1, 0), 0)),
                  const(1, D_MODEL),
                  pl.BlockSpec((None, D_MODEL, tn), lambda i, j: (l, 0, j)),
                  pl.BlockSpec((None, D_MODEL, tn), lambda i, j: (l, 0, nj + j)),
                  pl.BlockSpec((3, tn), lambda i, j: (0, j)),
                  pl.BlockSpec((3, tn), lambda i, j: (0, nj + j)),
                  pl.BlockSpec((1, tn), lambda i, j: (0, j)),
                  pl.BlockSpec((1, tn), lambda i, j: (0, nj + j)),
                  pl.BlockSpec((None, tn, D_MODEL), lambda i, j: (l, j, 0)),
                  const(1, D_MODEL)],
        out_specs=pl.BlockSpec((tm, D_MODEL), lambda i, j: (i, 0)),
        out_shape=jax.ShapeDtypeStruct((m, D_MODEL), F32),
        scratch_shapes=[pltpu.VMEM((tm + FFN_HALO, D_MODEL), BF16),
                        pltpu.VMEM((tm + FFN_HALO, tn), F32),
                        pltpu.VMEM((tm + FFN_HALO, tn), F32)],
        compiler_params=_cparams(("parallel", "arbitrary")),
        name="ffn",
    )(x2, x2, gpre, w_up_bf, w_up_bf, conv_w, conv_w, conv_b, conv_b, w_down_bf, gpost)


def kernel(x, w_in, w_out, mix_beta, hg_lb_logits, hg_onorm, s5_lam_re, s5_lam_im, s5_log_dt, s5_b_re,
           s5_b_im, s5_c_re, s5_c_im, s5_d, s5_w_glu, s5_b_glu, ffn_w_up, ffn_conv_w, ffn_conv_b,
           ffn_w_down, g_pre_mix, g_post_mix, g_pre_ffn, g_post_ffn):
    batch, seq, d = x.shape
    depth = w_in.shape[0]
    assert d == D_MODEL and seq % (S5_R * S5_L) == 0 and seq % TM_FFN == 0
    p = jax.nn.softmax(hg_lb_logits.astype(F32), axis=0)
    cum = jnp.cumsum(p, axis=0)
    lower_bounds = cum - cum[0:1]
    ret_tables = _ret_tables(seq)
    row = lambda a: a.astype(F32).reshape(1, -1)

    w_in_bf, w_out_bf, w_glu_bf = w_in.astype(BF16), w_out.astype(BF16), s5_w_glu.astype(BF16)
    w_up_bf, w_down_bf = ffn_w_up.astype(BF16), ffn_w_down.astype(BF16)
    s5_tabs = jax.vmap(_s5_tables)(s5_lam_re, s5_lam_im, s5_log_dt, s5_b_re, s5_b_im, s5_c_re, s5_c_im)
    x2 = x.reshape(batch * seq, d)
    for l in range(depth):
        o_hg, zu, o_rt = _mix(x2, row(g_pre_mix[l]), w_in_bf, l, row(lower_bounds[l]), row(hg_onorm[l]),
                              ret_tables, seq)
        y_s5 = _s5(zu, s5_tabs, l, batch, seq)
        x2 = _outproj(o_hg, y_s5, zu, o_rt, row(s5_d[l]), w_glu_bf, row(s5_b_glu[l]),
                      row(mix_beta[l]), w_out_bf, x2, row(g_post_mix[l]), l)
        x2 = _ffn(x2, row(g_pre_ffn[l]), w_up_bf, ffn_conv_w[l].astype(F32),
                  row(ffn_conv_b[l]), w_down_bf, row(g_post_ffn[l]), seq, l)
    return x2.reshape(batch, seq, d)
```

```python
import functools
import math

import jax
import jax.numpy as jnp
from jax import lax
from jax.experimental import pallas as pl
from jax.experimental.pallas import tpu as pltpu

F32 = jnp.float32
BF16 = jnp.bfloat16

D_MODEL = 2048
CHUNK = 64
HG_HEADS = 6
HG_D = 128
HG_W = HG_HEADS * HG_D
S5_GROUP_CH = 16
S5_W = 512
S5_GROUPS = 32
S5_STATE = 64
RET_HEADS = 6
RET_DK = 64
RET_DV = 128
RET_W = RET_HEADS * RET_DV
RET_QK = RET_HEADS * RET_DK
D_MIX = HG_W + S5_W + RET_W
ZH_W = 4 * HG_W
ZR_W = 2 * RET_QK + 2 * RET_W
N_IN = ZH_W + S5_W + ZR_W
D_FF = 5632
NORM_EPS = 1e-6
ROPE_BASE = 10000.0

LANES = 128
SUBL = 8
LOG2E = math.log2(math.e)
VMEM_LIMIT = 60000 * 1024

TM_IN = 256
TM_OUT = 512
TM_FFN = 1024
TN_FFN = 512
FFN_HALO = 16
FFN_STRIP = 256
HG_LEVELS = (32, 16, 8, 4, 2, 1)
HG_PAIR_W = 2 * HG_D
RET_CHUNK = 256
S5_L = 8
S5_R = 1024
S5_OCT = 4
S5_OS = 8 * S5_STATE


def _cparams(sem):
    return pltpu.CompilerParams(dimension_semantics=sem, vmem_limit_bytes=VMEM_LIMIT)


def _rms_f32(x):
    return x * lax.rsqrt(jnp.mean(x * x, axis=-1, keepdims=True) + NORM_EPS)


def _hgrn_level_exponents(log_f):
    nv = CHUNK // SUBL
    w = log_f.shape[-1]
    x = log_f.reshape(nv, SUBL, w)
    sub = lax.broadcasted_iota(jnp.int32, (nv, SUBL, w), 1)
    loc = x
    for d in (1, 2, 4):
        loc = loc + jnp.where(sub >= d, pltpu.roll(loc, d, 1), 0.0)
    bc = lambda a, r: jnp.broadcast_to(a[:, r:r + 1, :], a.shape)
    tot = bc(loc, SUBL - 1)
    suf = tot - loc
    tots = [tot[j] for j in range(nv)]
    locs = [loc[j] for j in range(nv)]
    sufs = [suf[j] for j in range(nv)]

    def span(js):
        acc = None
        for j in js:
            acc = tots[j] if acc is None else acc + tots[j]
        return acc

    def plus(a, s):
        return a if s is None else a + s

    b = jnp.stack([plus(locs[j], span(range(j))) for j in range(nv)])
    b_rev = jnp.stack([plus(sufs[j], span(range(j + 1, nv))) for j in range(nv)])
    levels = []
    for h in HG_LEVELS:
        if h >= SUBL:
            g = h // SUBL
            rows = []
            for j in range(nv):
                blk = j // g
                if blk % 2 == 1:
                    rows.append(plus(locs[j], span(range(blk * g, j))))
                else:
                    rows.append(plus(sufs[j], span(range(j + 1, (blk + 1) * g))))
            levels.append(jnp.stack(rows))
        elif h == 4:
            ref = bc(loc, 3)
            levels.append(jnp.where(sub >= 4, loc - ref, ref - loc))
        elif h == 2:
            ref = jnp.where(sub < 4, bc(loc, 1), bc(loc, 5))
            levels.append(jnp.where((sub & 3) >= 2, loc - ref, ref - loc))
        else:
            levels.append(jnp.where((sub & 1) == 1, x, 0.0))
    flat = lambda a: a.reshape(CHUNK, w)
    return flat(b), flat(b_rev), [flat(l) for l in levels]


def _hgrn_consts():
    t2 = lax.broadcasted_iota(jnp.int32, (CHUNK, HG_D), 0)
    s2 = lax.broadcasted_iota(jnp.int32, (CHUNK, HG_D), 1) & (CHUNK - 1)
    top_bit = t2 ^ s2
    level_masks = [(top_bit >= h) & (top_bit < 2 * h) & (t2 > s2) for h in HG_LEVELS]
    diag_mask = t2 == s2
    r = lax.broadcasted_iota(jnp.int32, (HG_PAIR_W, HG_PAIR_W), 0) >= HG_D
    c = lax.broadcasted_iota(jnp.int32, (HG_PAIR_W, HG_PAIR_W), 1) >= HG_D
    return level_masks, diag_mask, r == c, jnp.zeros((CHUNK, HG_D), BF16)


def _hgrn_chunk(q_ref, f_ref, i_ref, g_ref, lb, onorm, o_ref, st_ref, ch, consts):
    level_masks, diag_mask, same_head, zeros_bf = consts
    log_lb = jnp.log(lb)
    log_1mlb = jnp.log1p(-lb)
    one_mlb = 1.0 - lb
    nt = (((1,), (1,)), ((), ()))
    tn = (((0,), (0,)), ((), ()))

    def block_diag_rows(a):
        return jnp.concatenate([jnp.concatenate([a[:, :HG_D], zeros_bf], axis=1),
                                jnp.concatenate([zeros_bf, a[:, HG_D:]], axis=1)], axis=0)

    rows = pl.ds(ch * CHUNK, CHUNK)
    qr = q_ref[rows, :]
    fr = f_ref[rows, :]
    gr = g_ref[rows, :]
    v_bf = i_ref[rows, :].astype(BF16)
    q = qr * jax.nn.sigmoid(qr)
    e = jnp.exp(-jnp.abs(fr))
    log_sig = jnp.minimum(fr, 0.0) - jnp.log(1.0 + e)
    cc = log_1mlb + log_sig
    log_f = jnp.maximum(log_lb, cc) + jnp.log(1.0 + jnp.exp(-jnp.abs(log_lb - cc)))
    k = one_mlb * (jnp.where(fr >= 0.0, e, 1.0) / (1.0 + e))
    b, b_rev, level_exps = _hgrn_level_exponents(log_f * LOG2E)
    e_b = jnp.exp2(b)
    q_bf = q.astype(BF16)
    k_bf = k.astype(BF16)
    st = st_ref[...]
    o = lax.dot_general((q * e_b).astype(BF16), st.astype(BF16), nt, preferred_element_type=F32)
    k_dec = (k * jnp.exp2(b_rev)).astype(BF16)
    kv = lax.dot_general(v_bf, k_dec, tn, preferred_element_type=F32)
    st_ref[...] = st * e_b[CHUNK - 1:CHUNK, :] + jnp.where(same_head, kv, 0.0)
    a = jnp.where(diag_mask, lax.dot_general(q_bf, block_diag_rows(k_bf), nt,
                                             preferred_element_type=F32), 0.0)
    for d_h, mask in zip(level_exps, level_masks):
        e_h = jnp.exp2(d_h).astype(BF16)
        p = lax.dot_general(q_bf * e_h, block_diag_rows(k_bf * e_h), nt, preferred_element_type=F32)
        a = jnp.where(mask, p, a)
    o = o + jnp.dot(a.astype(BF16), block_diag_rows(v_bf), preferred_element_type=F32)
    gate = gr * jax.nn.sigmoid(gr)
    for hh in range(2):
        cols = slice(hh * HG_D, (hh + 1) * HG_D)
        o_ref[rows, cols] = _rms_f32(o[:, cols]) * onorm[:, cols] * gate[:, cols]


def _s5_tables(lam_re, lam_im, log_dt, b_re, b_im, c_re, c_im):
    hp = lax.Precision.HIGHEST
    L = S5_L
    dt = jnp.exp(log_dt.astype(F32))[:, None]
    lr = jnp.minimum(lam_re.astype(F32), -1e-4)
    li = lam_im.astype(F32)
    mag = jnp.exp(lr * dt)
    ar, ai = mag * jnp.cos(li * dt), mag * jnp.sin(li * dt)
    den = lr * lr + li * li
    nr_, ni_ = ar - 1.0, ai
    cr, ci = (nr_ * lr + ni_ * li) / den, (ni_ * lr - nr_ * li) / den
    br_, bi_ = b_re.astype(F32), b_im.astype(F32)
    bbr = cr[..., None] * br_ - ci[..., None] * bi_
    bbi = cr[..., None] * bi_ + ci[..., None] * br_
    cre, cim = c_re.astype(F32), c_im.astype(F32)
    jj = jnp.arange(L + 1, dtype=F32)[:, None, None]
    pmag = jnp.exp(lr * dt * jj)
    pr, pi = pmag * jnp.cos(li * dt * jj), pmag * jnp.sin(li * dt * jj)

    G, P, CH = S5_GROUPS, S5_STATE, S5_GROUP_CH
    ein = functools.partial(jnp.einsum, precision=hp)
    kj = (ein('gcp,jgp,gpd->jgcd', cre, pr[:L], bbr) - ein('gcp,jgp,gpd->jgcd', cre, pi[:L], bbi)
          - ein('gcp,jgp,gpd->jgcd', cim, pr[:L], bbi) - ein('gcp,jgp,gpd->jgcd', cim, pi[:L], bbr))
    eye8 = jnp.eye(8, dtype=F32)
    kb = jnp.einsum('joicd,ik->joidkc', kj.reshape(L, S5_OCT, 8, CH, CH), eye8)
    kb = kb.reshape(L, S5_OCT, LANES, LANES)
    s_idx = jnp.arange(L)[:, None]
    t_idx = jnp.arange(L)[None, :]
    lag = t_idx - s_idx
    toe = jnp.where((lag >= 0)[:, :, None, None, None], kb[jnp.clip(lag, 0, L - 1)], 0.0)
    w_toe = toe.transpose(2, 0, 3, 1, 4).reshape(S5_OCT, L * LANES, L * LANES)

    prs, pis = pr[L - 1 - jnp.arange(L)], pi[L - 1 - jnp.arange(L)]
    vre = prs[..., None] * bbr[None] - pis[..., None] * bbi[None]
    vim = prs[..., None] * bbi[None] + pis[..., None] * bbr[None]

    def v_lay(z):
        z = z.reshape(L, S5_OCT, 8, P, CH)
        z = jnp.einsum('soipd,ik->soidkp', z, eye8)
        return z.reshape(L, S5_OCT, LANES, S5_OS).transpose(1, 0, 2, 3).reshape(S5_OCT, L * LANES, S5_OS)

    w_in = jnp.concatenate([v_lay(vre), v_lay(vim)], axis=-1)

    prt, pit = pr[1:L + 1], pi[1:L + 1]
    mre = cre[None] * prt[:, :, None, :] - cim[None] * pit[:, :, None, :]
    mim = -(cre[None] * pit[:, :, None, :] + cim[None] * prt[:, :, None, :])

    def m_lay(z):
        z = z.reshape(L, S5_OCT, 8, CH, P)
        z = jnp.einsum('toicp,ik->toipkc', z, eye8)
        return z.reshape(L, S5_OCT, S5_OS, LANES).transpose(1, 2, 0, 3).reshape(S5_OCT, S5_OS, L * LANES)

    w_out = jnp.concatenate([m_lay(mre), m_lay(mim)], axis=1)
    a_l = jnp.concatenate([pr[L].reshape(S5_OCT, 1, S5_OS), pi[L].reshape(S5_OCT, 1, S5_OS)], axis=-1)
    return w_toe.astype(BF16), w_in.astype(BF16), w_out.astype(BF16), a_l


def _s5_kernel(u_ref, wt_ref, wi_ref, wo_ref, al_ref, y_ref, s_ref, v_ref, sp_ref):
    @pl.when(pl.program_id(2) == 0)
    def _():
        s_ref[...] = jnp.zeros_like(s_ref)

    L, R = S5_L, S5_R
    lhs = jnp.concatenate([u_ref[pl.ds(s, R, stride=L), :] for s in range(L)], axis=1).astype(BF16)
    y_intra = jnp.dot(lhs, wt_ref[0], preferred_element_type=F32)
    v_ref[...] = jnp.dot(lhs, wi_ref[0], preferred_element_type=F32)
    ar = al_ref[0, :, :S5_OS]
    ai = al_ref[0, :, S5_OS:]

    def step(n, carry):
        sr, si = carry
        sp_ref[pl.ds(n, 1), :S5_OS] = sr
        sp_ref[pl.ds(n, 1), S5_OS:] = si
        vr = v_ref[pl.ds(n, 1), :S5_OS]
        vi = v_ref[pl.ds(n, 1), S5_OS:]
        return (ar * sr - ai * si + vr, ar * si + ai * sr + vi)

    sr, si = lax.fori_loop(0, R, step, (s_ref[:, :S5_OS], s_ref[:, S5_OS:]), unroll=8)
    s_ref[:, :S5_OS] = sr
    s_ref[:, S5_OS:] = si
    y = y_intra + jnp.dot(sp_ref[...].astype(BF16), wo_ref[0], preferred_element_type=F32)
    for t in range(L):
        y_ref[pl.ds(t, R, stride=L), :] = y[:, t * LANES:(t + 1) * LANES]


def _s5(zu, tables, l, batch, seq):
    w_toe, w_in, w_out, a_l = tables
    m = zu.shape[0]
    rows = S5_R * S5_L
    nr = seq // rows
    full = lambda shp: pl.BlockSpec((None, 1) + shp, lambda o, b, r: (l, o, 0, 0))
    return pl.pallas_call(
        _s5_kernel,
        grid=(S5_OCT, batch, nr),
        in_specs=[pl.BlockSpec((rows, LANES), lambda o, b, r: (b * nr + r, o)),
                  full((S5_L * LANES, S5_L * LANES)), full((S5_L * LANES, 2 * S5_OS)),
                  full((2 * S5_OS, S5_L * LANES)), full((1, 2 * S5_OS))],
        out_specs=pl.BlockSpec((rows, LANES), lambda o, b, r: (b * nr + r, o)),
        out_shape=jax.ShapeDtypeStruct((m, S5_W), F32),
        scratch_shapes=[pltpu.VMEM((1, 2 * S5_OS), F32),
                        pltpu.VMEM((S5_R, 2 * S5_OS), F32),
                        pltpu.VMEM((S5_R, 2 * S5_OS), F32)],
        compiler_params=_cparams(("parallel", "parallel", "arbitrary")),
        name="s5",
    )(zu, w_toe, w_in, w_out, a_l)


def _ret_tables(seq):
    C = RET_CHUNK
    pos = jnp.arange(seq, dtype=F32)
    inv_freq = 1.0 / (ROPE_BASE ** jnp.linspace(0.0, 1.0, RET_DK // 2, dtype=F32))
    ang = pos[:, None] * inv_freq[None, :]
    cos, sin = jnp.cos(ang), jnp.sin(ang)
    cos_t = jnp.concatenate([cos, cos, cos, cos], axis=1)
    sin_t = jnp.concatenate([-sin, sin, -sin, sin], axis=1)
    log_g = jnp.log(1.0 - 2.0 ** (-5.0 - jnp.arange(RET_HEADS, dtype=F32)))
    idx = jnp.arange(C, dtype=F32)
    rel = idx[:, None] - idx[None, :]
    dmat = jnp.where(rel[None] >= 0, jnp.exp(jnp.maximum(rel, 0.0)[None] * log_g[:, None, None]), 0.0)
    lane = jnp.arange(LANES)
    head_mask = ((lane[None, :] // RET_DK) == (jnp.arange(RET_HEADS) % 2)[:, None]).astype(F32)
    gq = jnp.exp((idx + 1.0)[None, :] * log_g[:, None])
    gk = jnp.exp((C - 1.0 - idx)[None, :] * log_g[:, None])
    qm = gq[:, :, None] * head_mask[:, None, :]
    km = gk[:, :, None] * head_mask[:, None, :]
    gc = jnp.exp(C * log_g)
    gc_t = jnp.broadcast_to(gc[:, None, None], (RET_HEADS, 1, LANES))
    return cos_t, sin_t, dmat, head_mask[:, None, :], qm, km, gc_t


def _ret_rows(q_ref, k_ref, v_ref, g_ref, cos_t, sin_t, dm_ref, hm_ref, qm_ref, km_ref, gc_ref, o_ref, st_ref):
    lane = lax.broadcasted_iota(jnp.int32, (RET_CHUNK, LANES), 1)
    first_half = (lane % RET_DK) < (RET_DK // 2)
    nt = (((1,), (1,)), ((), ()))
    tn = (((0,), (0,)), ((), ()))

    def rope(z):
        swapped = jnp.where(first_half, pltpu.roll(z, LANES - RET_DK // 2, 1), pltpu.roll(z, RET_DK // 2, 1))
        return z * cos_t + swapped * sin_t

    for tile in range(RET_QK // LANES):
        cols = slice(tile * LANES, (tile + 1) * LANES)
        q_t = rope(q_ref[:, cols])
        k_t = rope(k_ref[:, cols]) * (RET_DK ** -0.5)
        k_bf = k_t.astype(BF16)
        for half in range(2):
            h = 2 * tile + half
            vcols = slice(h * RET_DV, (h + 1) * RET_DV)
            v_bf = v_ref[:, vcols].astype(BF16)
            q_h = (q_t * hm_ref[h]).astype(BF16)
            scores = lax.dot_general(q_h, k_bf, nt, preferred_element_type=F32) * dm_ref[h]
            o = jnp.dot(scores.astype(BF16), v_bf, preferred_element_type=F32)
            st = st_ref[h]
            o = o + jnp.dot((q_t * qm_ref[h]).astype(BF16), st.astype(BF16), preferred_element_type=F32)
            k_dec = (k_t * km_ref[h]).astype(BF16)
            st_ref[h] = st * gc_ref[h] + lax.dot_general(k_dec, v_bf, tn, preferred_element_type=F32)
            gr = g_ref[:, vcols]
            o_ref[:, vcols] = _rms_f32(o) * (gr * jax.nn.sigmoid(gr))


def _mix_kernel(x_ref, g_ref, w_ref, lb_ref, on_ref, cos_ref, sin_ref, dm_ref, hm_ref, qm_ref, km_ref, gc_ref,
                ohg_ref, zu_ref, ort_ref, zh_ref, zr_ref, hst_ref, rst_ref, *, tiles_per_seq):
    @pl.when(pl.program_id(0) % tiles_per_seq == 0)
    def _():
        hst_ref[...] = jnp.zeros_like(hst_ref)
        rst_ref[...] = jnp.zeros_like(rst_ref)

    h = (_rms_f32(x_ref[...]) * g_ref[...]).astype(BF16)

    def proj(c0, n):
        return jnp.dot(h, w_ref[:, c0:c0 + n], preferred_element_type=F32)

    zu_ref[...] = proj(ZH_W, S5_W)
    pair_views = []
    for pair in range(HG_HEADS // 2):
        views = []
        for kk in range(4):
            c0 = kk * HG_W + pair * HG_PAIR_W
            zh_ref[:, c0:c0 + HG_PAIR_W] = proj(c0, HG_PAIR_W)
            views.append(zh_ref.at[:, c0:c0 + HG_PAIR_W])
        pair_views.append(views)
    zr_ref[...] = proj(ZH_W + S5_W, ZR_W)
    onorm = on_ref[...]
    consts = _hgrn_consts()
    for ch in range(TM_IN // CHUNK):
        for pair in range(HG_HEADS // 2):
            pc = slice(pair * HG_PAIR_W, (pair + 1) * HG_PAIR_W)
            _hgrn_chunk(*pair_views[pair], lb_ref[:, pc], onorm, ohg_ref.at[:, pc], hst_ref.at[pair], ch,
                        consts)
    _ret_rows(zr_ref.at[:, :RET_QK], zr_ref.at[:, RET_QK:2 * RET_QK],
              zr_ref.at[:, 2 * RET_QK:2 * RET_QK + RET_W], zr_ref.at[:, 2 * RET_QK + RET_W:],
              cos_ref[...], sin_ref[...], dm_ref, hm_ref, qm_ref, km_ref, gc_ref, ort_ref, rst_ref)


def _mix(x2, gain, w_bf, l, lb, onorm, ret_tables, seq):
    cos_t, sin_t, dmat, hmask, qm, km, gc_t = ret_tables
    m = x2.shape[0]
    assert TM_IN == RET_CHUNK and TM_IN % CHUNK == 0 and seq % TM_IN == 0
    tiles_per_seq = seq // TM_IN
    onorm2 = jnp.concatenate([onorm, onorm], axis=1)
    row = lambda w: pl.BlockSpec((TM_IN, w), lambda i: (i, 0))
    const2 = lambda a: pl.BlockSpec(a.shape, lambda i: (0, 0))
    const3 = lambda a: pl.BlockSpec(a.shape, lambda i: (0, 0, 0))
    pos = pl.BlockSpec((TM_IN, LANES), lambda i: (i % tiles_per_seq, 0))
    return pl.pallas_call(
        functools.partial(_mix_kernel, tiles_per_seq=tiles_per_seq),
        grid=(m // TM_IN,),
        in_specs=[row(D_MODEL), const2(gain),
                  pl.BlockSpec((None, D_MODEL, N_IN), lambda i: (l, 0, 0), pipeline_mode=pl.Buffered(1)),
                  const2(lb), const2(onorm2), pos, pos,
                  const3(dmat), const3(hmask), const3(qm), const3(km), const3(gc_t)],
        out_specs=[row(HG_W), row(S5_W), row(RET_W)],
        out_shape=[jax.ShapeDtypeStruct((m, HG_W), F32),
                   jax.ShapeDtypeStruct((m, S5_W), F32),
                   jax.ShapeDtypeStruct((m, RET_W), F32)],
        scratch_shapes=[pltpu.VMEM((TM_IN, ZH_W), F32),
                        pltpu.VMEM((TM_IN, ZR_W), F32),
                        pltpu.VMEM((HG_HEADS // 2, HG_PAIR_W, HG_PAIR_W), F32),
                        pltpu.VMEM((RET_HEADS, LANES, RET_DV), F32)],
        compiler_params=_cparams(("arbitrary",)),
        name="mix",
    )(x2, gain, w_bf, lb, onorm2, cos_t, sin_t, dmat, hmask, qm, km, gc_t)


def _outproj_kernel(oh_ref, ys_ref, zu_ref, or_ref, d_ref, wg_ref, bg_ref, beta_ref, w_ref, x_ref, g_ref,
                    out_ref):
    beta = beta_ref[...]
    y = ys_ref[...] + d_ref[...] * zu_ref[...]
    y = 0.5 * y * (1.0 + lax.erf(y * (2.0 ** -0.5)))
    glu = jnp.dot(y.astype(BF16), wg_ref[...], preferred_element_type=F32) + bg_ref[...]
    o_s5 = y * jax.nn.sigmoid(glu)
    acc = jnp.dot((oh_ref[...] * beta[:, :HG_W]).astype(BF16), w_ref[:HG_W, :], preferred_element_type=F32)
    acc += jnp.dot((o_s5 * beta[:, HG_W:HG_W + S5_W]).astype(BF16), w_ref[HG_W:HG_W + S5_W, :],
                   preferred_element_type=F32)
    acc += jnp.dot((or_ref[...] * beta[:, HG_W + S5_W:]).astype(BF16), w_ref[HG_W + S5_W:, :],
                   preferred_element_type=F32)
    out_ref[...] = x_ref[...] + _rms_f32(acc) * g_ref[...]


def _outproj(o_hg, y_s5, zu, o_rt, d_skip, w_glu_bf, b_glu, beta, w_bf, x2, gain, l):
    m = x2.shape[0]
    row = lambda w: pl.BlockSpec((TM_OUT, w), lambda i: (i, 0))
    const = lambda r, c: pl.BlockSpec((r, c), lambda i: (0, 0))
    layer = lambda r, c: pl.BlockSpec((None, r, c), lambda i: (l, 0, 0))
    return pl.pallas_call(
        _outproj_kernel,
        grid=(m // TM_OUT,),
        in_specs=[row(HG_W), row(S5_W), row(S5_W), row(RET_W),
                  const(1, S5_W), layer(S5_W, S5_W), const(1, S5_W), const(1, D_MIX),
                  layer(D_MIX, D_MODEL), row(D_MODEL), const(1, D_MODEL)],
        out_specs=row(D_MODEL),
        out_shape=jax.ShapeDtypeStruct((m, D_MODEL), F32),
        compiler_params=_cparams(("parallel",)),
        name="outproj",
    )(o_hg, y_s5, zu, o_rt, d_skip, w_glu_bf, b_glu, beta, w_bf, x2, gain)


def _ffn_kernel(x_ref, xh_ref, gpre_ref, wg_ref, wv_ref, cwg_ref, cwv_ref, cbg_ref, cbv_ref, wd_ref,
                gpost_ref, out_ref, h_ref, ug_ref, uv_ref, *, tiles_per_seq):
    i = pl.program_id(0)
    j = pl.program_id(1)

    @pl.when(j == 0)
    def _():
        gpre = gpre_ref[...]
        for r0 in range(0, TM_FFN, FFN_STRIP):
            h_ref[pl.ds(FFN_HALO + r0, FFN_STRIP), :] = (
                _rms_f32(x_ref[pl.ds(r0, FFN_STRIP), :]) * gpre).astype(BF16)
        halo = (_rms_f32(xh_ref[...]) * gpre).astype(BF16)
        first = (i % tiles_per_seq) == 0
        h_ref[:FFN_HALO, :] = jnp.where(first, jnp.zeros_like(halo), halo)
        out_ref[...] = jnp.zeros_like(out_ref)

    edges = [0] + list(range(FFN_HALO + FFN_STRIP, TM_FFN + FFN_HALO + 1, FFN_STRIP))
    for lo, hi in zip(edges[:-1], edges[1:]):
        h = h_ref[lo:hi, :]
        ug_ref[lo:hi, :] = jnp.dot(h, wg_ref[...], preferred_element_type=F32)
        uv_ref[lo:hi, :] = jnp.dot(h, wv_ref[...], preferred_element_type=F32)
    cwg, cwv, cbg, cbv = cwg_ref[...], cwv_ref[...], cbg_ref[...], cbv_ref[...]

    def conv(u_ref, cw, cb, r0):
        taps = [u_ref[pl.ds(FFN_HALO - 2 + tap + r0, FFN_STRIP), :] for tap in range(3)]
        return cw[0:1] * taps[0] + cw[1:2] * taps[1] + cw[2:3] * taps[2] + cb

    for s in range(TM_FFN // FFN_STRIP):
        r0 = s * FFN_STRIP
        gate = conv(ug_ref, cwg, cbg, r0)
        val = conv(uv_ref, cwv, cbv, r0)
        act = (gate * jax.nn.sigmoid(gate) * val).astype(BF16)
        out_ref[pl.ds(r0, FFN_STRIP), :] += jnp.dot(act, wd_ref[...], preferred_element_type=F32)

    @pl.when(j == pl.num_programs(1) - 1)
    def _():
        gpost = gpost_ref[...]
        for r0 in range(0, TM_FFN, FFN_STRIP):
            rr = pl.ds(r0, FFN_STRIP)
            out_ref[rr, :] = x_ref[rr, :] + _rms_f32(out_ref[rr, :]) * gpost


def _ffn(x2, gpre, w_up_bf, conv_w, conv_b, w_down_bf, gpost, seq, l):
    m = x2.shape[0]
    tm, tn = TM_FFN, TN_FFN
    nj = D_FF // tn
    hb = tm // FFN_HALO
    const = lambda r, c: pl.BlockSpec((r, c), lambda i, j: (0, 0))
    return pl.pallas_call(
        functools.partial(_ffn_kernel, tiles_per_seq=seq // tm),
        grid=(m // tm, nj),
        in_specs=[pl.BlockSpec((tm, D_MODEL), lambda i, j: (i, 0)),
                  pl.BlockSpec((FFN_HALO, D_MODEL), lambda i, j: (jnp.maximum(i * hb - 1, 0), 0)),
                  const(1, D_MODEL),
                  pl.BlockSpec((None, D_MODEL, tn), lambda i, j: (l, 0, j)),
                  pl.BlockSpec((None, D_MODEL, tn), lambda i, j: (l, 0, nj + j)),
                  pl.BlockSpec((3, tn), lambda i, j: (0, j)),
                  pl.BlockSpec((3, tn), lambda i, j: (0, nj + j)),
                  pl.BlockSpec((1, tn), lambda i, j: (0, j)),
                  pl.BlockSpec((1, tn), lambda i, j: (0, nj + j)),
                  pl.BlockSpec((None, tn, D_MODEL), lambda i, j: (l, j, 0)),
                  const(1, D_MODEL)],
        out_specs=pl.BlockSpec((tm, D_MODEL), lambda i, j: (i, 0)),
        out_shape=jax.ShapeDtypeStruct((m, D_MODEL), F32),
        scratch_shapes=[pltpu.VMEM((tm + FFN_HALO, D_MODEL), BF16),
                        pltpu.VMEM((tm + FFN_HALO, tn), F32),
                        pltpu.VMEM((tm + FFN_HALO, tn), F32)],
        compiler_params=_cparams(("parallel", "arbitrary")),
        name="ffn",
    )(x2, x2, gpre, w_up_bf, w_up_bf, conv_w, conv_w, conv_b, conv_b, w_down_bf, gpost)


def kernel(x, w_in, w_out, mix_beta, hg_lb_logits, hg_onorm, s5_lam_re, s5_lam_im, s5_log_dt, s5_b_re,
           s5_b_im, s5_c_re, s5_c_im, s5_d, s5_w_glu, s5_b_glu, ffn_w_up, ffn_conv_w, ffn_conv_b,
           ffn_w_down, g_pre_mix, g_post_mix, g_pre_ffn, g_post_ffn):
    batch, seq, d = x.shape
    depth = w_in.shape[0]
    assert d == D_MODEL and seq % (S5_R * S5_L) == 0 and seq % TM_FFN == 0
    p = jax.nn.softmax(hg_lb_logits.astype(F32), axis=0)
    cum = jnp.cumsum(p, axis=0)
    lower_bounds = cum - cum[0:1]
    ret_tables = _ret_tables(seq)
    row = lambda a: a.astype(F32).reshape(1, -1)

    w_in_bf, w_out_bf, w_glu_bf = w_in.astype(BF16), w_out.astype(BF16), s5_w_glu.astype(BF16)
    w_up_bf, w_down_bf = ffn_w_up.astype(BF16), ffn_w_down.astype(BF16)
    s5_tabs = jax.vmap(_s5_tables)(s5_lam_re, s5_lam_im, s5_log_dt, s5_b_re, s5_b_im, s5_c_re, s5_c_im)
    x2 = x.reshape(batch * seq, d)
    for l in range(depth):
        o_hg, zu, o_rt = _mix(x2, row(g_pre_mix[l]), w_in_bf, l, row(lower_bounds[l]), row(hg_onorm[l]),
                              ret_tables, seq)
        y_s5 = _s5(zu, s5_tabs, l, batch, seq)
        x2 = _outproj(o_hg, y_s5, zu, o_rt, row(s5_d[l]), w_glu_bf, row(s5_b_glu[l]),
                      row(mix_beta[l]), w_out_bf, x2, row(g_post_mix[l]), l)
        x2 = _ffn(x2, row(g_pre_ffn[l]), w_up_bf, ffn_conv_w[l].astype(F32),
                  row(ffn_conv_b[l]), w_down_bf, row(g_post_ffn[l]), seq, l)
    return x2.reshape(batch, seq, d)
```
